```python
import jax, jax.numpy as jnp
from jax import lax
import numpy as np

D_MODEL = 2048
BATCH = 1
SEQ = 16384
DEPTH = 1
DEC_BATCH = 2
DEC_SEQ = 16384
PAST_LEN = 128

N_HEADS_A = 8
HEAD_K = 128
HEAD_V = 128
WIDTH_AK = N_HEADS_A * HEAD_K
WIDTH_AV = N_HEADS_A * HEAD_V
SHORT_CONV = 5
CHUNK = 64
WIDTH_B = 1024
CONV_B = 31
N_EXPERTS = 16
CAPACITY_FACTOR = 2
D_FF_EXPERT = 1024
EPS = 1e-6

QKV_W = 2 * WIDTH_AK + WIDTH_AV
OFF_Z = QKV_W
OFF_BETA = OFF_Z + WIDTH_AV
OFF_ALPHA = OFF_BETA + 2 * N_HEADS_A
OFF_GLU = OFF_ALPHA + 2 * N_HEADS_A
OFF_GATE = OFF_GLU + 2 * WIDTH_B
N_IN = OFF_GATE + 2 * D_MODEL

kernel_name = "hybrid_bidir_gdn_conformer_ec_moe"


def rmsnorm(x, g):
    x32 = x.astype(jnp.float32)
    y = x32 * lax.rsqrt(jnp.mean(x32 * x32, axis=-1, keepdims=True) + EPS)
    return (y * g.astype(jnp.float32)).astype(x.dtype)


def layernorm(x, g, b):
    x32 = x.astype(jnp.float32)
    mu = jnp.mean(x32, axis=-1, keepdims=True)
    xc = x32 - mu
    y = xc * lax.rsqrt(jnp.mean(xc * xc, axis=-1, keepdims=True) + EPS)
    return (y * g.astype(jnp.float32) + b.astype(jnp.float32)).astype(x.dtype)


def depthwise_conv(x, w):
    k = w.shape[0]
    return lax.conv_general_dilated(
        x, w[:, None, :].astype(x.dtype), window_strides=(1,),
        padding=[(k // 2, k // 2)], dimension_numbers=("NWC", "WIO", "NWC"),
        feature_group_count=x.shape[-1])


def l2norm(x):
    return x * lax.rsqrt(jnp.sum(x * x, axis=-1, keepdims=True) + EPS)


def gated_delta_chunked(q, k, v, g, beta):
    b, h, l, dk = q.shape
    dv = v.shape[-1]
    n = l // CHUNK
    q = q * (dk ** -0.5)
    q = q.reshape(b, h, n, CHUNK, dk)
    k = k.reshape(b, h, n, CHUNK, dk)
    v = v.reshape(b, h, n, CHUNK, dv)
    beta = beta.reshape(b, h, n, CHUNK)
    g = jnp.cumsum(g.reshape(b, h, n, CHUNK), axis=-1)
    tril = jnp.tril(jnp.ones((CHUNK, CHUNK), dtype=bool))
    strict = jnp.tril(jnp.ones((CHUNK, CHUNK), dtype=bool), -1)
    diff = g[..., :, None] - g[..., None, :]
    decay = jnp.where(tril, jnp.exp(jnp.where(tril, diff, 0.0)), 0.0)
    kb = k * beta[..., None]
    lmat = jnp.where(strict, jnp.einsum("bhncd,bhnsd->bhncs", kb, k) * decay, 0.0)
    a_mat = lmat + jnp.eye(CHUNK, dtype=lmat.dtype)
    rhs = jnp.concatenate([v * beta[..., None], kb * jnp.exp(g)[..., None]], axis=-1)
    sol = lax.linalg.triangular_solve(a_mat, rhs, left_side=True, lower=True, unit_diagonal=True)
    u = sol[..., :dv]
    w = sol[..., dv:]
    qk = jnp.where(tril, jnp.einsum("bhncd,bhnsd->bhncs", q, k) * decay, 0.0)

    def step(s, inp):
        qc, kc, uc, wc, gc, qkc = inp
        v_new = uc - jnp.einsum("bhcd,bhde->bhce", wc, s)
        o = (jnp.einsum("bhcd,bhde->bhce", qc * jnp.exp(gc)[..., None], s)
             + jnp.einsum("bhcs,bhse->bhce", qkc, v_new))
        g_last = gc[..., -1]
        s = (s * jnp.exp(g_last)[..., None, None]
             + jnp.einsum("bhcd,bhce->bhde", kc * jnp.exp(g_last[..., None] - gc)[..., None], v_new))
        return s, o

    mv = lambda t: jnp.moveaxis(t, 2, 0)
    s0 = jnp.zeros((b, h, dk, dv), jnp.float32)
    _, o = lax.scan(step, s0, (mv(q), mv(k), mv(u), mv(w), mv(g), mv(qk)))
    return jnp.moveaxis(o, 0, 2).reshape(b, h, l, dv)


def token_mixers(h, w_in, conv_a_w, a_log, dt_bias, head_norm, w_proj_a,
                 conv_b_w, conv_b_b, ln_b_g, ln_b_b, w_proj_b, w_out):
    bsz, l, _ = h.shape
    p = h @ w_in
    qkv = jax.nn.silu(depthwise_conv(p[..., :QKV_W], conv_a_w)).astype(jnp.float32)
    to_heads = lambda t, d: t.reshape(bsz, l, N_HEADS_A, d).transpose(0, 2, 1, 3)
    q = l2norm(to_heads(qkv[..., :WIDTH_AK], HEAD_K))
    k = l2norm(to_heads(qkv[..., WIDTH_AK:2 * WIDTH_AK], HEAD_K))
    v = to_heads(qkv[..., 2 * WIDTH_AK:], HEAD_V)
    z = p[..., OFF_Z:OFF_BETA]
    beta = jax.nn.sigmoid(p[..., OFF_BETA:OFF_ALPHA].astype(jnp.float32))
    beta = beta.reshape(bsz, l, 2, N_HEADS_A).transpose(2, 0, 3, 1)
    alpha = p[..., OFF_ALPHA:OFF_GLU].astype(jnp.float32)
    alpha = alpha.reshape(bsz, l, 2, N_HEADS_A).transpose(2, 0, 3, 1)
    g = (-jnp.exp(a_log.astype(jnp.float32))[:, None, :, None]
         * jax.nn.softplus(alpha + dt_bias.astype(jnp.float32)[:, None, :, None]))
    flip = lambda t: jnp.flip(t, axis=2)
    o_fwd = gated_delta_chunked(q, k, v, g[0], beta[0])
    o_bwd = flip(gated_delta_chunked(flip(q), flip(k), flip(v), flip(g[1]), flip(beta[1])))
    o = (o_fwd + o_bwd).transpose(0, 2, 1, 3)
    o = o * lax.rsqrt(jnp.mean(o * o, axis=-1, keepdims=True) + EPS) * head_norm.astype(jnp.float32)
    o = o.reshape(bsz, l, WIDTH_AV).astype(h.dtype) * jax.nn.silu(z)
    y_a = o @ w_proj_a
    glu = p[..., OFF_GLU:OFF_GATE]
    u = glu[..., :WIDTH_B] * jax.nn.sigmoid(glu[..., WIDTH_B:])
    u = depthwise_conv(u, conv_b_w) + conv_b_b
    u = jax.nn.silu(layernorm(u, ln_b_g, ln_b_b))
    y_b = u @ w_proj_b
    gates = jax.nn.sigmoid(p[..., OFF_GATE:].astype(jnp.float32))
    merged = gates[..., :D_MODEL] * y_a.astype(jnp.float32) + gates[..., D_MODEL:] * y_b.astype(jnp.float32)
    return merged.astype(h.dtype) @ w_out


def expert_choice_ffn(h, w_router, w_gate, w_up, w_down):
    shp = h.shape
    t = h.reshape(-1, D_MODEL)
    n_tok = t.shape[0]
    cap = max(1, CAPACITY_FACTOR * n_tok // N_EXPERTS)
    aff = jax.nn.softmax((t @ w_router).astype(jnp.float32), axis=-1)
    top_aff, idx = lax.top_k(aff.T, cap)
    xe = t[idx]
    a = jnp.einsum("ecd,edf->ecf", xe, w_gate)
    b = jnp.einsum("ecd,edf->ecf", xe, w_up)
    ye = jnp.einsum("ecf,efd->ecd", jax.nn.silu(a) * b, w_down)
    ye = (ye.astype(jnp.float32) * top_aff[..., None]).astype(t.dtype)
    out = jnp.zeros_like(t).at[idx.reshape(-1)].add(ye.reshape(-1, D_MODEL))
    return out.reshape(shp)


def trunk(x, norm_mix, w_in, conv_a_w, a_log, dt_bias, head_norm, w_proj_a,
          conv_b_w, conv_b_b, ln_b_g, ln_b_b, w_proj_b, w_out,
          norm_ffn, w_router, w_gate, w_up, w_down, norm_final):
    for i in range(DEPTH):
        h = rmsnorm(x, norm_mix[i])
        x = x + token_mixers(h, w_in[i], conv_a_w[i], a_log[i], dt_bias[i], head_norm[i], w_proj_a[i],
                             conv_b_w[i], conv_b_b[i], ln_b_g[i], ln_b_b[i], w_proj_b[i], w_out[i])
        h = rmsnorm(x, norm_ffn[i])
        x = x + expert_choice_ffn(h, w_router[i], w_gate[i], w_up[i], w_down[i])
    return rmsnorm(x, norm_final)


def setup_inputs(seed: int = 0) -> dict:
    key = jax.random.key(seed)
    ks = jax.random.split(key, 24)
    nrm = lambda k, shape, fan_in: jax.random.normal(k, shape, jnp.float32) * (fan_in ** -0.5)
    gain = lambda k, shape: 1.0 + 0.02 * jax.random.normal(k, shape, jnp.float32)
    dt = jnp.exp(jax.random.uniform(ks[5], (DEPTH, 2, N_HEADS_A), jnp.float32, np.log(1e-3), np.log(1e-1)))
    return {
        "x_prompt": jax.random.normal(ks[0], (BATCH, SEQ, D_MODEL), jnp.float32),
        "x_sample": jax.random.normal(ks[1], (DEC_BATCH, DEC_SEQ, D_MODEL), jnp.float32),
        "norm_mix": gain(ks[2], (DEPTH, D_MODEL)),
        "w_in": nrm(ks[3], (DEPTH, D_MODEL, N_IN), D_MODEL),
        "conv_a_w": nrm(ks[4], (DEPTH, SHORT_CONV, QKV_W), SHORT_CONV),
        "a_log": jnp.log(jax.random.uniform(ks[6], (DEPTH, 2, N_HEADS_A), jnp.float32, 1.0, 16.0)),
        "dt_bias": dt + jnp.log(-jnp.expm1(-dt)),
        "head_norm": gain(ks[7], (DEPTH, HEAD_V)),
        "w_proj_a": nrm(ks[8], (DEPTH, WIDTH_AV, D_MODEL), WIDTH_AV),
        "conv_b_w": nrm(ks[9], (DEPTH, CONV_B, WIDTH_B), CONV_B),
        "conv_b_b": 0.02 * jax.random.normal(ks[10], (DEPTH, WIDTH_B), jnp.float32),
        "ln_b_g": gain(ks[11], (DEPTH, WIDTH_B)),
        "ln_b_b": 0.02 * jax.random.normal(ks[12], (DEPTH, WIDTH_B), jnp.float32),
        "w_proj_b": nrm(ks[13], (DEPTH, WIDTH_B, D_MODEL), WIDTH_B),
        "w_out": nrm(ks[14], (DEPTH, D_MODEL, D_MODEL), D_MODEL),
        "norm_ffn": gain(ks[15], (DEPTH, D_MODEL)),
        "w_router": nrm(ks[16], (DEPTH, D_MODEL, N_EXPERTS), D_MODEL),
        "w_gate": nrm(ks[17], (DEPTH, N_EXPERTS, D_MODEL, D_FF_EXPERT), D_MODEL),
        "w_up": nrm(ks[18], (DEPTH, N_EXPERTS, D_MODEL, D_FF_EXPERT), D_MODEL),
        "w_down": nrm(ks[19], (DEPTH, N_EXPERTS, D_FF_EXPERT, D_MODEL), D_FF_EXPERT),
        "norm_final": gain(ks[20], (D_MODEL,)),
    }


def reference(x_prompt, x_sample, norm_mix, w_in, conv_a_w, a_log, dt_bias, head_norm, w_proj_a,
              conv_b_w, conv_b_b, ln_b_g, ln_b_b, w_proj_b, w_out,
              norm_ffn, w_router, w_gate, w_up, w_down, norm_final):
    y_prompt = trunk(x_prompt, norm_mix, w_in, conv_a_w, a_log, dt_bias, head_norm, w_proj_a,
                     conv_b_w, conv_b_b, ln_b_g, ln_b_b, w_proj_b, w_out,
                     norm_ffn, w_router, w_gate, w_up, w_down, norm_final)
    y_sample = trunk(x_sample, norm_mix, w_in, conv_a_w, a_log, dt_bias, head_norm, w_proj_a,
                     conv_b_w, conv_b_b, ln_b_g, ln_b_b, w_proj_b, w_out,
                     norm_ffn, w_router, w_gate, w_up, w_down, norm_final)
    return (y_prompt, y_sample)
```

```python
import functools

import jax
import jax.numpy as jnp
from jax import lax
from jax.experimental import pallas as pl
from jax.experimental.pallas import tpu as pltpu

F32 = jnp.float32
BF16 = jnp.bfloat16
I32 = jnp.int32

HEAD_DIM = 128
CHUNK = 64
CHUNK_SHIFT = 6
CAPACITY_FACTOR = 2
EPS = 1e-6
LANES = 128
BF16_ROWS = 16
MIB = 1024 * 1024


def _cparams(sem, vmem_mib):
    return pltpu.CompilerParams(dimension_semantics=sem, vmem_limit_bytes=vmem_mib * MIB)


def _dot(a, b):
    return jnp.dot(a, b, preferred_element_type=F32)


def _dot_nt(a, b):
    return lax.dot_general(a, b, (((1,), (1,)), ((), ())), preferred_element_type=F32)


def _dot_tn(a, b):
    return lax.dot_general(a, b, (((0,), (0,)), ((), ())), preferred_element_type=F32)


def _split2(a):
    hi = a.astype(BF16)
    lo = (a - hi.astype(F32)).astype(BF16)
    return hi, lo


def _split3(a):
    hi = a.astype(BF16)
    r = a - hi.astype(F32)
    mid = r.astype(BF16)
    lo = (r - mid.astype(F32)).astype(BF16)
    return hi, mid, lo


def _dot_hi(a, b):
    ah, al = _split2(a)
    bh, bl = _split2(b)
    return _dot(ah, bh) + _dot(ah, bl) + _dot(al, bh)


def _sigmoid(x):
    return 1.0 / (1.0 + jnp.exp(-x))


def _silu(x):
    return x * _sigmoid(x)


def _softplus(x):
    return jnp.maximum(x, 0.0) + jnp.log(1.0 + jnp.exp(-jnp.abs(x)))


def _iota(shape, dim):
    return lax.broadcasted_iota(I32, shape, dim)


def _ones_bf16(shape):
    return jnp.ones(shape, BF16)


def _bf01(mask):
    return jnp.where(mask, 1.0, 0.0).astype(BF16)


def _inproj_kernel(x_ref, g_ref, wb_ref, ws_ref, p_ref, ab_ref, h_scr):
    @pl.when(pl.program_id(2) == 0)
    def _():
        rows = min(256, h_scr.shape[0])
        for r in range(h_scr.shape[0] // rows):
            x = x_ref[0, r * rows:(r + 1) * rows, :]
            ms = jnp.mean(x * x, axis=-1, keepdims=True)
            h = x * lax.rsqrt(ms + EPS) * g_ref[...]
            h_scr[r * rows:(r + 1) * rows, :] = h.astype(BF16)
            ab_ref[0, r * rows:(r + 1) * rows, :] = _dot_hi(h, ws_ref[...])

    p_ref[0] = _dot(h_scr[...], wb_ref[...]).astype(BF16)


def _inproj(x, g, w_big, w_small, tn):
    b, l, d = x.shape
    nbig = w_big.shape[1]
    nsmall = w_small.shape[1]
    tm = min(1024, l)
    return pl.pallas_call(
        _inproj_kernel,
        grid=(b, l // tm, nbig // tn),
        in_specs=[
            pl.BlockSpec((1, tm, d), lambda bi, i, j: (bi, i, 0)),
            pl.BlockSpec((1, d), lambda bi, i, j: (0, 0)),
            pl.BlockSpec((d, tn), lambda bi, i, j: (0, j)),
            pl.BlockSpec((d, nsmall), lambda bi, i, j: (0, 0)),
        ],
        out_specs=[
            pl.BlockSpec((1, tm, tn), lambda bi, i, j: (bi, i, j)),
            pl.BlockSpec((1, tm, nsmall), lambda bi, i, j: (bi, i, 0)),
        ],
        out_shape=[
            jax.ShapeDtypeStruct((b, l, nbig), BF16),
            jax.ShapeDtypeStruct((b, l, nsmall), F32),
        ],
        scratch_shapes=[pltpu.VMEM((tm, d), BF16)],
        compiler_params=_cparams(("parallel", "parallel", "arbitrary"), 48),
        name="inproj",
    )(x, g, w_big, w_small)


SUBLANES = 8


def _conv_shifts(ntaps):
    first = BF16_ROWS - ntaps // 2
    return sorted({(first + t) % SUBLANES for t in range(ntaps)})


def _conv_fill(scr, prev, main, nxt, i, last, tl, ntaps):
    hr = BF16_ROWS
    shifts = _conv_shifts(ntaps)
    rows = tl + 2 * hr
    base = shifts.index(0)
    scr[base, 0:hr, :] = jnp.where(i > 0, prev, 0.0)
    scr[base, hr:hr + tl, :] = main
    scr[base, hr + tl:rows, :] = jnp.where(i < last, nxt, 0.0)
    full = scr[base]
    for c, s in enumerate(shifts):
        if s:
            scr[c] = pltpu.roll(full, rows - s, axis=0)


def _conv_rows(scr, w, r0, rb, ntaps):
    shifts = _conv_shifts(ntaps)
    first = BF16_ROWS - ntaps // 2
    acc = None
    for t in range(ntaps):
        off = first + t
        s = off % SUBLANES
        win = scr[shifts.index(s), pl.ds(r0 + (off - s), rb), :]
        term = win * w[t:t + 1, :]
        acc = term if acc is None else acc + term
    return acc


def _prep_kernel(pm_ref, pp_ref, pn_ref, cw_ref, o_ref, scr, *, tl, rb, ntaps):
    i = pl.program_id(1)
    j = pl.program_id(2)
    last = pl.num_programs(1) - 1
    _conv_fill(scr, pp_ref[0].astype(F32), pm_ref[0].astype(F32), pn_ref[0].astype(F32),
               i, last, tl, ntaps)
    w = cw_ref[...]
    width = w.shape[1]
    nh = width // HEAD_DIM
    qscale = jnp.where(j == 0, HEAD_DIM ** -0.5, 1.0).astype(F32)
    is_qk = j < 2

    def body(r, carry):
        r0 = pl.multiple_of(r * rb, rb)
        y = _silu(_conv_rows(scr, w, r0, rb, ntaps))
        outs = []
        for h in range(nh):
            yh = y[:, h * HEAD_DIM:(h + 1) * HEAD_DIM]
            ss = jnp.sum(yh * yh, axis=-1, keepdims=True)
            fac = jnp.where(is_qk, lax.rsqrt(ss + EPS) * qscale, 1.0)
            outs.append(yh * fac)
        o_ref[0, 0, pl.ds(r0, rb), :] = jnp.concatenate(outs, axis=-1).astype(BF16)
        return carry

    lax.fori_loop(0, tl // rb, body, 0)


def _prep(p_big, conv_w, width):
    b, l, _ = p_big.shape
    tl = min(256, l)
    rb = 32
    hr = BF16_ROWS
    nhb = l // hr
    ntaps = conv_w.shape[0]
    assert ntaps // 2 < hr
    kern = functools.partial(_prep_kernel, tl=tl, rb=rb, ntaps=ntaps)
    return pl.pallas_call(
        kern,
        grid=(b, l // tl, 3),
        in_specs=[
            pl.BlockSpec((1, tl, width), lambda bi, i, j: (bi, i, j)),
            pl.BlockSpec((1, hr, width), lambda bi, i, j: (bi, jnp.maximum(i * (tl // hr) - 1, 0), j)),
            pl.BlockSpec((1, hr, width), lambda bi, i, j: (bi, jnp.minimum((i + 1) * (tl // hr), nhb - 1), j)),
            pl.BlockSpec((ntaps, width), lambda bi, i, j: (0, j)),
        ],
        out_specs=pl.BlockSpec((1, 1, tl, width), lambda bi, i, j: (j, bi, i, 0)),
        out_shape=jax.ShapeDtypeStruct((3, b, l, width), BF16),
        scratch_shapes=[pltpu.VMEM((len(_conv_shifts(ntaps)), tl + 2 * hr, width), F32)],
        compiler_params=_cparams(("parallel", "parallel", "arbitrary"), 32),
        name="prep",
    )(p_big, p_big, p_big, conv_w)


def _gates_kernel(ab_ref, at_ref, arow_ref, dtrow_ref, acol_ref, dtcol_ref,
                  gc_ref, bt_ref, gct_ref, *, tl, nh):
    ab = ab_ref[0]
    beta = _sigmoid(ab[:, :2 * nh])
    alpha = ab[:, 2 * nh:]
    g = -jnp.exp(arow_ref[...]) * _softplus(alpha + dtrow_ref[...])
    r = _iota((tl, tl), 0)
    c = _iota((tl, tl), 1)
    same = jnp.right_shift(r, CHUNK_SHIFT) == jnp.right_shift(c, CHUNK_SHIFT)
    m_f = _bf01(same & (c <= r))
    m_b = _bf01(same & (c >= r))
    g3 = _split3(g)
    gf = _dot(m_f, g3[0]) + _dot(m_f, g3[1]) + _dot(m_f, g3[2])
    gb = _dot(m_b, g3[0]) + _dot(m_b, g3[1]) + _dot(m_b, g3[2])
    gc_ref[0, 0] = gf[:, :nh]
    gc_ref[0, 1] = gb[:, nh:]
    bt_ref[0, 0] = beta[:, :nh]
    bt_ref[0, 1] = beta[:, nh:]
    nc = tl // CHUNK
    at = at_ref[0]
    gt = -jnp.exp(acol_ref[...])[None] * _softplus(at + dtcol_ref[...][None])
    gt2 = gt.reshape(nc * 2 * nh, CHUNK)
    rr = _iota((CHUNK, CHUNK), 0)
    cc = _iota((CHUNK, CHUNK), 1)
    u_f = _bf01(rr <= cc)
    u_b = _bf01(rr >= cc)
    t3 = _split3(gt2)
    cf = (_dot(t3[0], u_f) + _dot(t3[1], u_f) + _dot(t3[2], u_f)).reshape(nc, 2 * nh, CHUNK)
    cb = (_dot(t3[0], u_b) + _dot(t3[1], u_b) + _dot(t3[2], u_b)).reshape(nc, 2 * nh, CHUNK)
    gct_ref[0, 0] = cf[:, :nh, :]
    gct_ref[0, 1] = cb[:, nh:, :]


def _gates(ab, a_log, dt_bias, nh):
    b, l, _ = ab.shape
    tl = min(512, l)
    nc = tl // CHUNK
    at = ab[..., 2 * nh:].reshape(b, l // CHUNK, CHUNK, 2 * nh).swapaxes(-1, -2)
    arow = a_log.reshape(1, 2 * nh)
    dtrow = dt_bias.reshape(1, 2 * nh)
    acol = a_log.reshape(2 * nh, 1)
    dtcol = dt_bias.reshape(2 * nh, 1)
    kern = functools.partial(_gates_kernel, tl=tl, nh=nh)
    small = lambda shape: pl.BlockSpec(shape, lambda bi, i: (0, 0))
    return pl.pallas_call(
        kern,
        grid=(b, l // tl),
        in_specs=[
            pl.BlockSpec((1, tl, 4 * nh), lambda bi, i: (bi, i, 0)),
            pl.BlockSpec((1, nc, 2 * nh, CHUNK), lambda bi, i: (bi, i, 0, 0)),
            small((1, 2 * nh)), small((1, 2 * nh)), small((2 * nh, 1)), small((2 * nh, 1)),
        ],
        out_specs=[
            pl.BlockSpec((1, 2, tl, nh), lambda bi, i: (bi, 0, i, 0)),
            pl.BlockSpec((1, 2, tl, nh), lambda bi, i: (bi, 0, i, 0)),
            pl.BlockSpec((1, 2, nc, nh, CHUNK), lambda bi, i: (bi, 0, i, 0, 0)),
        ],
        out_shape=[
            jax.ShapeDtypeStruct((b, 2, l, nh), F32),
            jax.ShapeDtypeStruct((b, 2, l, nh), F32),
            jax.ShapeDtypeStruct((b, 2, l // CHUNK, nh, CHUNK), F32),
        ],
        compiler_params=_cparams(("parallel", "parallel"), 32),
        name="gates",
    )(ab, at, arow, dtrow, acol, dtcol)


def _gdn_kernel(q_ref, k_ref, v_ref, gc_ref, bt_ref, gct_ref, o_ref, s_scr, *, tl, nh):
    d = pl.program_id(1)
    i = pl.program_id(2)

    @pl.when(i == 0)
    def _():
        s_scr[...] = jnp.zeros_like(s_scr)

    nc = tl // CHUNK
    fwd = d == 0
    row = _iota((CHUNK, CHUNK), 0)
    col = _iota((CHUNK, CHUNK), 1)
    ahead = (row - col) * jnp.where(fwd, 1, -1)
    m_incl = ahead >= 0
    m_strict = ahead > 0
    eye = jnp.where(row == col, 1.0, 0.0).astype(F32)

    def chunk(ci, carry):
        c = jnp.where(fwd, ci, nc - 1 - ci)
        r0 = pl.multiple_of(c * CHUNK, CHUNK)
        gcols = gc_ref[0, 0, pl.ds(r0, CHUNK), :]
        bcols = bt_ref[0, 0, pl.ds(r0, CHUNK), :]
        grows = gct_ref[0, 0, c]
        glast = jnp.where(fwd, gcols[CHUNK - 1:CHUNK, :], gcols[0:1, :])
        for h in range(nh):
            sl = slice(h * HEAD_DIM, (h + 1) * HEAD_DIM)
            q = q_ref[0, 0, pl.ds(r0, CHUNK), sl].astype(F32)
            k = k_ref[0, 0, pl.ds(r0, CHUNK), sl].astype(F32)
            v = v_ref[0, 0, pl.ds(r0, CHUNK), sl].astype(F32)
            gcol = gcols[:, h:h + 1]
            beta = bcols[:, h:h + 1]
            grow = grows[h:h + 1, :]
            gl = glast[:, h:h + 1]
            eg = jnp.exp(gcol)
            ekl = jnp.exp(gl - gcol)
            dec = jnp.where(m_incl, jnp.exp(jnp.where(m_incl, gcol - grow, 0.0)), 0.0)
            kb = k * beta
            k16 = k.astype(BF16)
            kk = _dot_nt(kb.astype(BF16), k16)
            lm = jnp.where(m_strict, kk * dec, 0.0)
            qk = _dot_nt(q.astype(BF16), k16) * dec
            pw = -lm
            tm = eye + pw
            for _ in range(5):
                p16 = pw.astype(BF16)
                pw = _dot(p16, p16)
                tm = tm + _dot(tm.astype(BF16), pw.astype(BF16))
            rhs = jnp.concatenate([v * beta, kb * eg], axis=-1).astype(BF16)
            sol = _dot(tm.astype(BF16), rhs)
            u = sol[:, :HEAD_DIM]
            w = sol[:, HEAD_DIM:]
            s = s_scr[h]
            s16 = s.astype(BF16)
            wq = jnp.concatenate([w, q * eg], axis=0).astype(BF16)
            ws = _dot(wq, s16)
            v_new = u - ws[:CHUNK]
            vn16 = v_new.astype(BF16)
            o = ws[CHUNK:] + _dot(qk.astype(BF16), vn16)
            s_scr[h] = s * jnp.exp(gl) + _dot_tn((k * ekl).astype(BF16), vn16)
            o_ref[0, 0, pl.ds(r0, CHUNK), sl] = o.astype(BF16)
        return carry

    lax.fori_loop(0, nc, chunk, 0)


def _gdn(qkv, gcum, beta, gcum_t, nh):
    _, b, l, width = qkv.shape
    tl = min(512, l)
    nt = l // tl
    nc = tl // CHUNK
    li = lambda d, i: jnp.where(d == 0, i, nt - 1 - i)
    kern = functools.partial(_gdn_kernel, tl=tl, nh=nh)
    qkv_spec = lambda which: pl.BlockSpec((1, 1, tl, width), lambda bi, d, i: (which, bi, li(d, i), 0))
    return pl.pallas_call(
        kern,
        grid=(b, 2, nt),
        in_specs=[
            qkv_spec(0), qkv_spec(1), qkv_spec(2),
            pl.BlockSpec((1, 1, tl, nh), lambda bi, d, i: (bi, d, li(d, i), 0)),
            pl.BlockSpec((1, 1, tl, nh), lambda bi, d, i: (bi, d, li(d, i), 0)),
            pl.BlockSpec((1, 1, nc, nh, CHUNK), lambda bi, d, i: (bi, d, li(d, i), 0, 0)),
        ],
        out_specs=pl.BlockSpec((1, 1, tl, width), lambda bi, d, i: (d, bi, li(d, i), 0)),
        out_shape=jax.ShapeDtypeStruct((2, b, l, width), BF16),
        scratch_shapes=[pltpu.VMEM((nh, HEAD_DIM, HEAD_DIM), F32)],
        compiler_params=_cparams(("parallel", "parallel", "arbitrary"), 32),
        name="gdn",
    )(qkv, qkv, qkv, gcum, beta, gcum_t)


def _mix_kernel(of_ref, ob_ref, z_ref, gm_ref, gp_ref, gn_ref, ga_ref, gb_ref,
                hn_ref, cw_ref, cb_ref, lg_ref, lb_ref, wa_ref, wb_ref,
                m_ref, scr, act, *, tl, rb, ntaps, wb_width):
    i = pl.program_id(1)
    last = pl.num_programs(1) - 1

    o = of_ref[0, 0].astype(F32) + ob_ref[0, 0].astype(F32)
    width = o.shape[1]
    nh = width // HEAD_DIM
    hn = hn_ref[...]
    parts = []
    for h in range(nh):
        oh = o[:, h * HEAD_DIM:(h + 1) * HEAD_DIM]
        ms = jnp.mean(oh * oh, axis=-1, keepdims=True)
        parts.append(oh * lax.rsqrt(ms + EPS) * hn)
    on = jnp.concatenate(parts, axis=-1)
    ya_in = (on * _silu(z_ref[0].astype(F32))).astype(BF16)
    y_a = _dot(ya_in, wa_ref[...])

    def glu(blk):
        blk = blk.astype(F32)
        return blk[:, :wb_width] * _sigmoid(blk[:, wb_width:])

    _conv_fill(scr, glu(gp_ref[0]), glu(gm_ref[0]), glu(gn_ref[0]), i, last, tl, ntaps)
    cw = cw_ref[...]
    cb = cb_ref[...]
    lg = lg_ref[...]
    lb = lb_ref[...]

    def body(r, carry):
        r0 = pl.multiple_of(r * rb, rb)
        acc = _conv_rows(scr, cw, r0, rb, ntaps) + cb
        mu = jnp.mean(acc, axis=-1, keepdims=True)
        xc = acc - mu
        var = jnp.mean(xc * xc, axis=-1, keepdims=True)
        y = xc * lax.rsqrt(var + EPS) * lg + lb
        act[pl.ds(r0, rb), :] = _silu(y).astype(BF16)
        return carry

    lax.fori_loop(0, tl // rb, body, 0)
    y_b = _dot(act[...], wb_ref[...])

    g_a = _sigmoid(ga_ref[0].astype(F32))
    g_b = _sigmoid(gb_ref[0].astype(F32))
    m_ref[0] = (g_a * y_a + g_b * y_b).astype(BF16)


def _mix(o2, p_big, head_norm, conv_w, conv_b, ln_g, ln_b, w_a, w_b, width, wb_width, d):
    _, b, l, _ = o2.shape
    tl = min(256, l)
    rb = 32
    hr = BF16_ROWS
    nhb = l // hr
    ntaps = conv_w.shape[0]
    z_blk = 3
    glu_blk = (4 * width) // (2 * wb_width)
    gate_blk = (4 * width + 2 * wb_width) // d
    assert (4 * width) % (2 * wb_width) == 0 and (4 * width + 2 * wb_width) % d == 0
    kern = functools.partial(_mix_kernel, tl=tl, rb=rb, ntaps=ntaps, wb_width=wb_width)
    const = lambda shape: pl.BlockSpec(shape, lambda bi, i: (0, 0))
    return pl.pallas_call(
        kern,
        grid=(b, l // tl),
        in_specs=[
            pl.BlockSpec((1, 1, tl, width), lambda bi, i: (0, bi, i, 0)),
            pl.BlockSpec((1, 1, tl, width), lambda bi, i: (1, bi, i, 0)),
            pl.BlockSpec((1, tl, width), lambda bi, i: (bi, i, z_blk)),
            pl.BlockSpec((1, tl, 2 * wb_width), lambda bi, i: (bi, i, glu_blk)),
            pl.BlockSpec((1, hr, 2 * wb_width),
                         lambda bi, i: (bi, jnp.maximum(i * (tl // hr) - 1, 0), glu_blk)),
            pl.BlockSpec((1, hr, 2 * wb_width),
                         lambda bi, i: (bi, jnp.minimum((i + 1) * (tl // hr), nhb - 1), glu_blk)),
            pl.BlockSpec((1, tl, d), lambda bi, i: (bi, i, gate_blk)),
            pl.BlockSpec((1, tl, d), lambda bi, i: (bi, i, gate_blk + 1)),
            const((1, HEAD_DIM)),
            const((ntaps, wb_width)), const((1, wb_width)), const((1, wb_width)), const((1, wb_width)),
            const((width, d)), const((wb_width, d)),
        ],
        out_specs=pl.BlockSpec((1, tl, d), lambda bi, i: (bi, i, 0)),
        out_shape=jax.ShapeDtypeStruct((b, l, d), BF16),
        scratch_shapes=[pltpu.VMEM((len(_conv_shifts(ntaps)), tl + 2 * hr, wb_width), F32),
                        pltpu.VMEM((tl, wb_width), BF16)],
        compiler_params=_cparams(("parallel", "parallel"), 48),
        name="mix",
    )(o2, o2, p_big, p_big, p_big, p_big, p_big, p_big,
      head_norm, conv_w, conv_b, ln_g, ln_b, w_a, w_b)


def _outproj_kernel(m_ref, x_ref, wo_ref, nf_ref, wr_ref, x1_ref, h2_ref, aff_ref):
    x1 = x_ref[0] + _dot(m_ref[0], wo_ref[...])
    x1_ref[0] = x1
    ms = jnp.mean(x1 * x1, axis=-1, keepdims=True)
    h2 = x1 * lax.rsqrt(ms + EPS) * nf_ref[...]
    h2_ref[0] = h2
    logits = _dot_hi(h2, wr_ref[...])
    mx = jnp.max(logits, axis=-1, keepdims=True)
    ex = jnp.exp(logits - mx)
    aff_ref[0] = ex / jnp.sum(ex, axis=-1, keepdims=True)


def _outproj(merged, x, w_out, norm_ffn, w_router):
    b, l, d = x.shape
    ne = w_router.shape[1]
    tl = min(256, l)
    const = lambda shape: pl.BlockSpec(shape, lambda bi, i: (0, 0))
    tok = lambda last: pl.BlockSpec((1, tl, last), lambda bi, i: (bi, i, 0))
    return pl.pallas_call(
        _outproj_kernel,
        grid=(b, l // tl),
        in_specs=[tok(d), tok(d), const((d, d)), const((1, d)), const((d, ne))],
        out_specs=[tok(d), tok(d), tok(ne)],
        out_shape=[
            jax.ShapeDtypeStruct((b, l, d), F32),
            jax.ShapeDtypeStruct((b, l, d), F32),
            jax.ShapeDtypeStruct((b, l, ne), F32),
        ],
        compiler_params=_cparams(("parallel", "parallel"), 48),
        name="outproj",
    )(merged, x, w_out, norm_ffn, w_router)


def _route_kernel(a_ref, idx_ref, pos_ref, rs_ref, *, cap, nrows):
    a = a_ref[0]
    bits = lax.bitcast_convert_type(a, I32)
    capf = jnp.float32(cap)

    def count(mask):
        return jnp.sum(jnp.where(mask, 1.0, 0.0), keepdims=True)

    thr = jnp.zeros((1, 1), I32)
    for bit in range(30, -1, -1):
        cand = thr | (1 << bit)
        thr = jnp.where(count(bits >= cand) >= capf, cand, thr)

    upper = _bf01(_iota((LANES, LANES), 0) <= _iota((LANES, LANES), 1))
    ones_ll = _ones_bf16((LANES, LANES))
    strict_lower = _bf01(_iota((nrows, nrows), 1) < _iota((nrows, nrows), 0))

    def prefix(m01):
        within = _dot(m01, upper)
        tot = _dot(m01, ones_ll)
        before = _dot(strict_lower, tot.astype(BF16))
        return within + before

    gt = bits > thr
    eq = bits == thr
    need = capf - count(gt)
    eq_rank = prefix(_bf01(eq))
    mask = gt | (eq & (eq_rank <= need))
    m01 = _bf01(mask)
    incl = prefix(m01)
    pos_ref[0] = jnp.where(mask, incl - 1.0, -1.0).astype(I32)

    ones_8l = _ones_bf16((8, LANES))
    tot_row = _dot_nt(ones_8l, m01)
    strict_upper = _bf01(_iota((nrows, nrows), 0) < _iota((nrows, nrows), 1))
    start_row = _dot(tot_row.astype(BF16), strict_upper)
    end_row = start_row + tot_row
    rs_ref[0] = start_row[0:1, :].astype(I32)

    s_col = _iota((cap, nrows), 0).astype(F32)
    g01 = _bf01((start_row[0:1, :] <= s_col) & (s_col < end_row[0:1, :]))
    hi = jnp.floor(incl * (1.0 / 256.0))
    lo = incl - 256.0 * hi
    rank_at = 256.0 * _dot(g01, hi.astype(BF16)) + _dot(g01, lo.astype(BF16))
    s_lane = _iota((cap, LANES), 0).astype(F32)
    ind = _bf01(rank_at <= s_lane)
    local_row = _dot_nt(ones_8l, ind)
    rvals = _iota((8, nrows), 1).astype(F32).astype(BF16)
    row_of = _dot_nt(rvals, g01)
    idx_ref[0] = (row_of[0:1, :] * float(LANES) + local_row[0:1, :]).astype(I32)


def _route(aff_t3, cap):
    ne, nrows, _ = aff_t3.shape
    assert nrows <= 256 and cap % LANES == 0
    kern = functools.partial(_route_kernel, cap=cap, nrows=nrows)
    return pl.pallas_call(
        kern,
        grid=(ne,),
        in_specs=[pl.BlockSpec((1, nrows, LANES), lambda e: (e, 0, 0))],
        out_specs=[
            pl.BlockSpec((1, 1, cap), lambda e: (e, 0, 0)),
            pl.BlockSpec((1, nrows, LANES), lambda e: (e, 0, 0)),
            pl.BlockSpec((1, 1, nrows), lambda e: (e, 0, 0)),
        ],
        out_shape=[
            jax.ShapeDtypeStruct((ne, 1, cap), I32),
            jax.ShapeDtypeStruct((ne, nrows, LANES), I32),
            jax.ShapeDtypeStruct((ne, 1, nrows), I32),
        ],
        compiler_params=_cparams(("parallel",), 48),
        name="route",
    )(aff_t3)


def _expert_kernel(idx_ref, h2_hbm, wg_ref, wu_ref, wd_ref, ye_ref, buf, sem, *, ts):
    nj = pl.num_programs(1)
    step = pl.program_id(0) * nj + pl.program_id(1)
    nsteps = pl.num_programs(0) * nj
    slot = step % 2

    def row_copy(st, sl, r):
        t = idx_ref[st * ts + r]
        return pltpu.make_async_copy(h2_hbm.at[pl.ds(t, 1), :], buf.at[sl, pl.ds(r, 1), :], sem.at[sl])

    def issue(st, sl):
        def body(r, carry):
            row_copy(st, sl, r).start()
            return carry
        lax.fori_loop(0, ts, body, 0)

    @pl.when(step == 0)
    def _():
        issue(0, 0)

    @pl.when(step + 1 < nsteps)
    def _():
        issue(step + 1, 1 - slot)

    def wait_body(r, carry):
        row_copy(step, slot, r).wait()
        return carry
    lax.fori_loop(0, ts, wait_body, 0)

    x = buf[slot].astype(BF16)
    a = _dot(x, wg_ref[0])
    b = _dot(x, wu_ref[0])
    hm = (_silu(a) * b).astype(BF16)
    ye_ref[...] = _dot(hm, wd_ref[0]).astype(BF16)


def _experts(idx_flat, h2, w_gate, w_up, w_down, cap):
    n, d = h2.shape
    ne, _, f = w_gate.shape
    ts = min(256, cap)
    nj = cap // ts
    kern = functools.partial(_expert_kernel, ts=ts)
    grid_spec = pltpu.PrefetchScalarGridSpec(
        num_scalar_prefetch=1,
        grid=(ne, nj),
        in_specs=[
            pl.BlockSpec(memory_space=pl.ANY),
            pl.BlockSpec((1, d, f), lambda e, j, idx: (e, 0, 0)),
            pl.BlockSpec((1, d, f), lambda e, j, idx: (e, 0, 0)),
            pl.BlockSpec((1, f, d), lambda e, j, idx: (e, 0, 0)),
        ],
        out_specs=pl.BlockSpec((ts, d), lambda e, j, idx: (e * nj + j, 0)),
        scratch_shapes=[pltpu.VMEM((2, ts, d), F32), pltpu.SemaphoreType.DMA((2,))],
    )
    return pl.pallas_call(
        kern,
        grid_spec=grid_spec,
        out_shape=jax.ShapeDtypeStruct((ne * cap, d), BF16),
        compiler_params=_cparams(("arbitrary", "arbitrary"), 56),
        name="experts",
    )(idx_flat, h2, w_gate, w_up, w_down)


def _combine_kernel(ts_ref, x1_ref, pos_ref, aff_ref, nf_ref, ye_hbm, y_ref, buf, sem,
                    *, tt, cap, ne):
    i = pl.program_id(0)
    nsteps = pl.num_programs(0)
    slot = i % 2
    total = ne * cap
    br = tt + BF16_ROWS

    def start_of(st, e):
        first = e * cap + ts_ref[st * ne + e]
        aligned = jnp.bitwise_and(first, -BF16_ROWS)
        return pl.multiple_of(jnp.minimum(aligned, total - br), BF16_ROWS)

    def copy(st, sl, e):
        src = start_of(st, e)
        return pltpu.make_async_copy(ye_hbm.at[pl.ds(src, br), :],
                                     buf.at[sl, pl.ds(e * br, br), :], sem.at[sl])

    def active(st, e):
        return ts_ref[(st + 1) * ne + e] > ts_ref[st * ne + e]

    def issue(st, sl):
        for e in range(ne):
            @pl.when(active(st, e))
            def _():
                copy(st, sl, e).start()

    @pl.when(i == 0)
    def _():
        buf[...] = jnp.zeros_like(buf)
        issue(0, 0)

    @pl.when(i + 1 < nsteps)
    def _():
        issue(i + 1, 1 - slot)

    for e in range(ne):
        @pl.when(active(i, e))
        def _():
            copy(i, slot, e).wait()

    pos = pos_ref[...]
    aff = aff_ref[...]
    lane_e = _iota((1, ne), 1)
    base = jnp.zeros((1, ne), I32)
    for e in range(ne):
        base = jnp.where(lane_e == e, start_of(i, e) - e * cap, base)
    rel = jnp.clip(pos - base, -1, br).astype(F32)
    kdim = ne * br
    c_idx = _iota((ne, kdim), 1)
    e_idx = _iota((ne, kdim), 0)
    expand = _bf01((c_idx >= e_idx * br) & (c_idx < (e_idx + 1) * br))
    e_of_c = jnp.sum(jnp.where(c_idx >= (e_idx + 1) * br, 1.0, 0.0), axis=0, keepdims=True)
    k_of_c = _iota((1, kdim), 1).astype(F32) - float(br) * e_of_c
    rel_x = _dot(rel.astype(BF16), expand)
    aff_x = _dot(aff.astype(BF16), expand)
    emat = jnp.where(rel_x == k_of_c, aff_x, 0.0).astype(BF16)
    out = x1_ref[...] + _dot(emat, buf[slot])
    ms = jnp.mean(out * out, axis=-1, keepdims=True)
    y_ref[...] = out * lax.rsqrt(ms + EPS) * nf_ref[...]


def _combine(tile_starts, x1, pos_tm, aff, norm_final, ye, cap):
    n, d = x1.shape
    ne = aff.shape[1]
    tt = LANES
    br = tt + BF16_ROWS
    assert cap >= br and cap % BF16_ROWS == 0
    kern = functools.partial(_combine_kernel, tt=tt, cap=cap, ne=ne)
    grid_spec = pltpu.PrefetchScalarGridSpec(
        num_scalar_prefetch=1,
        grid=(n // tt,),
        in_specs=[
            pl.BlockSpec((tt, d), lambda i, ts: (i, 0)),
            pl.BlockSpec((tt, ne), lambda i, ts: (i, 0)),
            pl.BlockSpec((tt, ne), lambda i, ts: (i, 0)),
            pl.BlockSpec((1, d), lambda i, ts: (0, 0)),
            pl.BlockSpec(memory_space=pl.ANY),
        ],
        out_specs=pl.BlockSpec((tt, d), lambda i, ts: (i, 0)),
        scratch_shapes=[pltpu.VMEM((2, ne * br, d), BF16), pltpu.SemaphoreType.DMA((2,))],
    )
    return pl.pallas_call(
        kern,
        grid_spec=grid_spec,
        out_shape=jax.ShapeDtypeStruct((n, d), F32),
        compiler_params=_cparams(("arbitrary",), 48),
        name="combine",
    )(tile_starts, x1, pos_tm, aff, norm_final, ye)


def _trunk(x, w, nh, width, wb_width):
    b, l, d = x.shape
    n = b * l
    ne = w["w_router"].shape[1]
    cap = max(1, CAPACITY_FACTOR * n // ne)

    p_big, ab = _inproj(x, w["norm_mix"], w["w_big"], w["w_small"], width)
    qkv = _prep(p_big, w["conv_a_w"], width)
    gcum, beta, gcum_t = _gates(ab, w["a_log"], w["dt_bias"], nh)
    o2 = _gdn(qkv, gcum, beta, gcum_t, nh)
    merged = _mix(o2, p_big, w["head_norm"], w["conv_b_w"], w["conv_b_b"], w["ln_b_g"], w["ln_b_b"],
                  w["w_proj_a"], w["w_proj_b"], width, wb_width, d)
    x1, h2, aff = _outproj(merged, x, w["w_out"], w["norm_ffn"], w["w_router"])

    aff = aff.reshape(n, ne)
    nrows = n // LANES
    idx, pos, row_start = _route(aff.T.reshape(ne, nrows, LANES), cap)
    idx_flat = idx.reshape(ne * cap)
    pos_tm = pos.reshape(ne, n).T
    tile_starts = jnp.concatenate(
        [row_start.reshape(ne, nrows).T, jnp.full((1, ne), cap, I32)], axis=0).reshape(-1)

    ye = _experts(idx_flat, h2.reshape(n, d), w["w_gate"], w["w_up"], w["w_down"], cap)
    y = _combine(tile_starts, x1.reshape(n, d), pos_tm, aff, w["norm_final"], ye, cap)
    return y.reshape(b, l, d)


def kernel(x_prompt, x_sample, norm_mix, w_in, conv_a_w, a_log, dt_bias, head_norm, w_proj_a,
           conv_b_w, conv_b_b, ln_b_g, ln_b_b, w_proj_b, w_out, norm_ffn, w_router, w_gate, w_up,
           w_down, norm_final):
    assert w_in.shape[0] == 1, "single layer"
    width = conv_a_w.shape[-1] // 3
    wb_width = conv_b_w.shape[-1]
    nh = width // HEAD_DIM
    off_small = 4 * width
    off_glu = off_small + 4 * nh
    w_in0 = w_in[0]
    row = lambda v: v.reshape(1, -1)
    w = {
        "norm_mix": row(norm_mix[0]),
        "w_big": jnp.concatenate([w_in0[:, :off_small], w_in0[:, off_glu:]], axis=1).astype(BF16),
        "w_small": w_in0[:, off_small:off_glu],
        "conv_a_w": conv_a_w[0],
        "a_log": a_log[0],
        "dt_bias": dt_bias[0],
        "head_norm": row(head_norm[0]),
        "w_proj_a": w_proj_a[0].astype(BF16),
        "conv_b_w": conv_b_w[0],
        "conv_b_b": row(conv_b_b[0]),
        "ln_b_g": row(ln_b_g[0]),
        "ln_b_b": row(ln_b_b[0]),
        "w_proj_b": w_proj_b[0].astype(BF16),
        "w_out": w_out[0].astype(BF16),
        "norm_ffn": row(norm_ffn[0]),
        "w_router": w_router[0],
        "w_gate": w_gate[0].astype(BF16),
        "w_up": w_up[0].astype(BF16),
        "w_down": w_down[0].astype(BF16),
        "norm_final": row(norm_final),
    }
    y_prompt = _trunk(x_prompt, w, nh, width, wb_width)
    y_sample = _trunk(x_sample, w, nh, width, wb_width)
    return (y_prompt, y_sample)
```

```python
import functools

import jax
import jax.numpy as jnp
from jax import lax
from jax.experimental import pallas as pl
from jax.experimental.pallas import tpu as pltpu

F32 = jnp.float32
BF16 = jnp.bfloat16
I32 = jnp.int32

HEAD_DIM = 128
CHUNK = 64
CHUNK_SHIFT = 6
CAPACITY_FACTOR = 2
EPS = 1e-6
LANES = 128
BF16_ROWS = 16
MIB = 1024 * 1024


def _cparams(sem, vmem_mib):
    return pltpu.CompilerParams(dimension_semantics=sem, vmem_limit_bytes=vmem_mib * MIB)


def _dot(a, b):
    return jnp.dot(a, b, preferred_element_type=F32)


def _dot_nt(a, b):
    return lax.dot_general(a, b, (((1,), (1,)), ((), ())), preferred_element_type=F32)


def _dot_tn(a, b):
    return lax.dot_general(a, b, (((0,), (0,)), ((), ())), preferred_element_type=F32)


def _split2(a):
    hi = a.astype(BF16)
    lo = (a - hi.astype(F32)).astype(BF16)
    return hi, lo


def _split3(a):
    hi = a.astype(BF16)
    r = a - hi.astype(F32)
    mid = r.astype(BF16)
    lo = (r - mid.astype(F32)).astype(BF16)
    return hi, mid, lo


def _dot_hi(a, b):
    ah, al = _split2(a)
    bh, bl = _split2(b)
    return _dot(ah, bh) + _dot(ah, bl) + _dot(al, bh)


def _sigmoid(x):
    return 0.5 * jnp.tanh(0.5 * x) + 0.5


def _silu(x):
    return x * _sigmoid(x)


def _softplus(x):
    return jnp.maximum(x, 0.0) + jnp.log(1.0 + jnp.exp(-jnp.abs(x)))


def _iota(shape, dim):
    return lax.broadcasted_iota(I32, shape, dim)


def _ones_bf16(shape):
    return jnp.ones(shape, BF16)


def _bf01(mask):
    return jnp.where(mask, 1.0, 0.0).astype(BF16)


def _inproj_kernel(x_ref, g_ref, wb_ref, ws_ref, p_ref, ab_ref, h_scr):
    @pl.when(pl.program_id(2) == 0)
    def _():
        rows = min(256, h_scr.shape[0])
        for r in range(h_scr.shape[0] // rows):
            x = x_ref[0, r * rows:(r + 1) * rows, :]
            ms = jnp.mean(x * x, axis=-1, keepdims=True)
            h = x * lax.rsqrt(ms + EPS) * g_ref[...]
            h_scr[r * rows:(r + 1) * rows, :] = h.astype(BF16)
            ab_ref[0, r * rows:(r + 1) * rows, :] = _dot_hi(h, ws_ref[...])

    p_ref[0] = _dot(h_scr[...], wb_ref[...]).astype(BF16)


def _inproj(x, g, w_big, w_small, tn):
    b, l, d = x.shape
    nbig = w_big.shape[1]
    nsmall = w_small.shape[1]
    tm = min(1024, l)
    return pl.pallas_call(
        _inproj_kernel,
        grid=(b, l // tm, nbig // tn),
        in_specs=[
            pl.BlockSpec((1, tm, d), lambda bi, i, j: (bi, i, 0)),
            pl.BlockSpec((1, d), lambda bi, i, j: (0, 0)),
            pl.BlockSpec((d, tn), lambda bi, i, j: (0, j)),
            pl.BlockSpec((d, nsmall), lambda bi, i, j: (0, 0)),
        ],
        out_specs=[
            pl.BlockSpec((1, tm, tn), lambda bi, i, j: (bi, i, j)),
            pl.BlockSpec((1, tm, nsmall), lambda bi, i, j: (bi, i, 0)),
        ],
        out_shape=[
            jax.ShapeDtypeStruct((b, l, nbig), BF16),
            jax.ShapeDtypeStruct((b, l, nsmall), F32),
        ],
        scratch_shapes=[pltpu.VMEM((tm, d), BF16)],
        compiler_params=_cparams(("parallel", "parallel", "arbitrary"), 48),
        name="inproj",
    )(x, g, w_big, w_small)


SUBLANES = 8


def _conv_shifts(ntaps):
    first = BF16_ROWS - ntaps // 2
    return sorted({(first + t) % SUBLANES for t in range(ntaps)})


def _conv_fill(scr, prev, main, nxt, i, last, tl, ntaps):
    hr = BF16_ROWS
    shifts = _conv_shifts(ntaps)
    rows = tl + 2 * hr
    base = shifts.index(0)
    scr[base, 0:hr, :] = jnp.where(i > 0, prev, 0.0)
    scr[base, hr:hr + tl, :] = main
    scr[base, hr + tl:rows, :] = jnp.where(i < last, nxt, 0.0)
    full = scr[base]
    for c, s in enumerate(shifts):
        if s:
            scr[c] = pltpu.roll(full, rows - s, axis=0)


def _conv_rows(scr, w, r0, rb, ntaps):
    shifts = _conv_shifts(ntaps)
    first = BF16_ROWS - ntaps // 2
    acc = None
    for t in range(ntaps):
        off = first + t
        s = off % SUBLANES
        win = scr[shifts.index(s), pl.ds(r0 + (off - s), rb), :]
        term = win * w[t:t + 1, :]
        acc = term if acc is None else acc + term
    return acc


def _prep_kernel(pm_ref, pp_ref, pn_ref, cw_ref, o_ref, scr, *, tl, rb, ntaps):
    i = pl.program_id(1)
    j = pl.program_id(2)
    last = pl.num_programs(1) - 1
    _conv_fill(scr, pp_ref[0].astype(F32), pm_ref[0].astype(F32), pn_ref[0].astype(F32),
               i, last, tl, ntaps)
    w = cw_ref[...]
    width = w.shape[1]
    nh = width // HEAD_DIM
    qscale = jnp.where(j == 0, HEAD_DIM ** -0.5, 1.0).astype(F32)
    is_qk = j < 2

    def body(r, carry):
        r0 = pl.multiple_of(r * rb, rb)
        y = _silu(_conv_rows(scr, w, r0, rb, ntaps))
        outs = []
        for h in range(nh):
            yh = y[:, h * HEAD_DIM:(h + 1) * HEAD_DIM]
            ss = jnp.sum(yh * yh, axis=-1, keepdims=True)
            fac = jnp.where(is_qk, lax.rsqrt(ss + EPS) * qscale, 1.0)
            outs.append(yh * fac)
        o_ref[0, 0, pl.ds(r0, rb), :] = jnp.concatenate(outs, axis=-1).astype(BF16)
        return carry

    lax.fori_loop(0, tl // rb, body, 0)


def _prep(p_big, conv_w, width):
    b, l, _ = p_big.shape
    tl = min(256, l)
    rb = 32
    hr = BF16_ROWS
    nhb = l // hr
    ntaps = conv_w.shape[0]
    assert ntaps // 2 < hr
    kern = functools.partial(_prep_kernel, tl=tl, rb=rb, ntaps=ntaps)
    return pl.pallas_call(
        kern,
        grid=(b, l // tl, 3),
        in_specs=[
            pl.BlockSpec((1, tl, width), lambda bi, i, j: (bi, i, j)),
            pl.BlockSpec((1, hr, width), lambda bi, i, j: (bi, jnp.maximum(i * (tl // hr) - 1, 0), j)),
            pl.BlockSpec((1, hr, width), lambda bi, i, j: (bi, jnp.minimum((i + 1) * (tl // hr), nhb - 1), j)),
            pl.BlockSpec((ntaps, width), lambda bi, i, j: (0, j)),
        ],
        out_specs=pl.BlockSpec((1, 1, tl, width), lambda bi, i, j: (j, bi, i, 0)),
        out_shape=jax.ShapeDtypeStruct((3, b, l, width), BF16),
        scratch_shapes=[pltpu.VMEM((len(_conv_shifts(ntaps)), tl + 2 * hr, width), F32)],
        compiler_params=_cparams(("parallel", "parallel", "arbitrary"), 32),
        name="prep",
    )(p_big, p_big, p_big, conv_w)


def _gates_kernel(ab_ref, at_ref, arow_ref, dtrow_ref, acol_ref, dtcol_ref,
                  gc_ref, bt_ref, gct_ref, *, tl, nh):
    ab = ab_ref[0]
    beta = _sigmoid(ab[:, :2 * nh])
    alpha = ab[:, 2 * nh:]
    g = -jnp.exp(arow_ref[...]) * _softplus(alpha + dtrow_ref[...])
    r = _iota((tl, tl), 0)
    c = _iota((tl, tl), 1)
    same = jnp.right_shift(r, CHUNK_SHIFT) == jnp.right_shift(c, CHUNK_SHIFT)
    m_f = _bf01(same & (c <= r))
    m_b = _bf01(same & (c >= r))
    g3 = _split3(g)
    gf = _dot(m_f, g3[0]) + _dot(m_f, g3[1]) + _dot(m_f, g3[2])
    gb = _dot(m_b, g3[0]) + _dot(m_b, g3[1]) + _dot(m_b, g3[2])
    gc_ref[0, 0] = gf[:, :nh]
    gc_ref[0, 1] = gb[:, nh:]
    bt_ref[0, 0] = beta[:, :nh]
    bt_ref[0, 1] = beta[:, nh:]
    nc = tl // CHUNK
    at = at_ref[0]
    gt = -jnp.exp(acol_ref[...])[None] * _softplus(at + dtcol_ref[...][None])
    gt2 = gt.reshape(nc * 2 * nh, CHUNK)
    rr = _iota((CHUNK, CHUNK), 0)
    cc = _iota((CHUNK, CHUNK), 1)
    u_f = _bf01(rr <= cc)
    u_b = _bf01(rr >= cc)
    t3 = _split3(gt2)
    cf = (_dot(t3[0], u_f) + _dot(t3[1], u_f) + _dot(t3[2], u_f)).reshape(nc, 2 * nh, CHUNK)
    cb = (_dot(t3[0], u_b) + _dot(t3[1], u_b) + _dot(t3[2], u_b)).reshape(nc, 2 * nh, CHUNK)
    gct_ref[0, 0] = cf[:, :nh, :]
    gct_ref[0, 1] = cb[:, nh:, :]


def _gates(ab, a_log, dt_bias, nh):
    b, l, _ = ab.shape
    tl = min(512, l)
    nc = tl // CHUNK
    at = ab[..., 2 * nh:].reshape(b, l // CHUNK, CHUNK, 2 * nh).swapaxes(-1, -2)
    arow = a_log.reshape(1, 2 * nh)
    dtrow = dt_bias.reshape(1, 2 * nh)
    acol = a_log.reshape(2 * nh, 1)
    dtcol = dt_bias.reshape(2 * nh, 1)
    kern = functools.partial(_gates_kernel, tl=tl, nh=nh)
    small = lambda shape: pl.BlockSpec(shape, lambda bi, i: (0, 0))
    return pl.pallas_call(
        kern,
        grid=(b, l // tl),
        in_specs=[
            pl.BlockSpec((1, tl, 4 * nh), lambda bi, i: (bi, i, 0)),
            pl.BlockSpec((1, nc, 2 * nh, CHUNK), lambda bi, i: (bi, i, 0, 0)),
            small((1, 2 * nh)), small((1, 2 * nh)), small((2 * nh, 1)), small((2 * nh, 1)),
        ],
        out_specs=[
            pl.BlockSpec((1, 2, tl, nh), lambda bi, i: (bi, 0, i, 0)),
            pl.BlockSpec((1, 2, tl, nh), lambda bi, i: (bi, 0, i, 0)),
            pl.BlockSpec((1, 2, nc, nh, CHUNK), lambda bi, i: (bi, 0, i, 0, 0)),
        ],
        out_shape=[
            jax.ShapeDtypeStruct((b, 2, l, nh), F32),
            jax.ShapeDtypeStruct((b, 2, l, nh), F32),
            jax.ShapeDtypeStruct((b, 2, l // CHUNK, nh, CHUNK), F32),
        ],
        compiler_params=_cparams(("parallel", "parallel"), 32),
        name="gates",
    )(ab, at, arow, dtrow, acol, dtcol)


def _gdn_kernel(q_ref, k_ref, v_ref, gc_ref, bt_ref, gct_ref, o_ref,
                s_scr, u_scr, wq_scr, ke_scr, qk_scr, *, tl, nh):
    d = pl.program_id(1)
    i = pl.program_id(2)

    @pl.when(i == 0)
    def _():
        s_scr[...] = jnp.zeros_like(s_scr)

    nc = tl // CHUNK
    fwd = d == 0
    row = _iota((CHUNK, CHUNK), 0)
    col = _iota((CHUNK, CHUNK), 1)
    ahead = (row - col) * jnp.where(fwd, 1, -1)
    m_incl = ahead >= 0
    m_strict = ahead > 0
    eye = jnp.where(row == col, 1.0, 0.0).astype(F32)

    heads = range(nh)
    sl = [slice(h * HEAD_DIM, (h + 1) * HEAD_DIM) for h in heads]

    def chunk_rows(ci):
        c = jnp.where(fwd, ci, nc - 1 - ci)
        return c, pl.multiple_of(c * CHUNK, CHUNK)

    def local_begin(ci):
        c, r0 = chunk_rows(ci)
        gcols = gc_ref[0, 0, pl.ds(r0, CHUNK), :]
        bcols = bt_ref[0, 0, pl.ds(r0, CHUNK), :]
        grows = gct_ref[0, 0, c]
        glast = jnp.where(fwd, gcols[CHUNK - 1:CHUNK, :], gcols[0:1, :])
        q = [q_ref[0, 0, pl.ds(r0, CHUNK), sl[h]] for h in heads]
        k16 = [k_ref[0, 0, pl.ds(r0, CHUNK), sl[h]] for h in heads]
        k = [k16[h].astype(F32) for h in heads]
        v = [v_ref[0, 0, pl.ds(r0, CHUNK), sl[h]].astype(F32) for h in heads]
        gcol = [gcols[:, h:h + 1] for h in heads]
        beta = [bcols[:, h:h + 1] for h in heads]
        gl = [glast[:, h:h + 1] for h in heads]
        kb = [k[h] * beta[h] for h in heads]
        kk = [_dot_nt(kb[h].astype(BF16), k16[h]) for h in heads]
        qk = [_dot_nt(q[h], k16[h]) for h in heads]
        return dict(grows=grows, q=q, k=k, v=v, gcol=gcol, beta=beta, gl=gl, kb=kb, kk=kk, qk=qk)

    def local_finish(st, slot):
        gcol, beta, kb = st["gcol"], st["beta"], st["kb"]
        dec = [jnp.where(m_incl, jnp.exp(jnp.where(m_incl, gcol[h] - st["grows"][h:h + 1, :], 0.0)), 0.0)
               for h in heads]
        pw = [jnp.where(m_strict, -(st["kk"][h] * dec[h]), 0.0) for h in heads]
        p16 = [pw[h].astype(BF16) for h in heads]
        tm = [eye + pw[h] for h in heads]
        for _ in range(5):
            pw = [_dot(p16[h], p16[h]) for h in heads]
            p16 = [pw[h].astype(BF16) for h in heads]
            tm = [tm[h] + _dot(tm[h].astype(BF16), p16[h]) for h in heads]
        eg = [jnp.exp(gcol[h]) for h in heads]
        rhs = [jnp.concatenate([st["v"][h] * beta[h], kb[h] * eg[h]], axis=-1).astype(BF16) for h in heads]
        sol = [_dot(tm[h].astype(BF16), rhs[h]) for h in heads]
        for h in heads:
            u_scr[slot, h] = sol[h][:, :HEAD_DIM]
            wq_scr[slot, h, 0:CHUNK, :] = sol[h][:, HEAD_DIM:].astype(BF16)
            wq_scr[slot, h, CHUNK:2 * CHUNK, :] = (st["q"][h].astype(F32) * eg[h]).astype(BF16)
            ke_scr[slot, h] = (st["k"][h] * jnp.exp(st["gl"][h] - gcol[h])).astype(BF16)
            qk_scr[slot, h] = (st["qk"][h] * dec[h]).astype(BF16)

    def scan_begin(slot):
        s = [s_scr[h] for h in heads]
        ws = [_dot(wq_scr[slot, h], s[h].astype(BF16)) for h in heads]
        return s, ws

    def scan_finish(ci, slot, s, ws):
        _, r0 = chunk_rows(ci)
        gcols = gc_ref[0, 0, pl.ds(r0, CHUNK), :]
        glast = jnp.where(fwd, gcols[CHUNK - 1:CHUNK, :], gcols[0:1, :])
        vn16 = [(u_scr[slot, h] - ws[h][:CHUNK]).astype(BF16) for h in heads]
        s_add = [_dot_tn(ke_scr[slot, h], vn16[h]) for h in heads]
        o_in = [_dot(qk_scr[slot, h], vn16[h]) for h in heads]
        for h in heads:
            s_scr[h] = s[h] * jnp.exp(glast[:, h:h + 1]) + s_add[h]
            o_ref[0, 0, pl.ds(r0, CHUNK), sl[h]] = (ws[h][CHUNK:] + o_in[h]).astype(BF16)

    def step(ci, slot):
        s, ws = scan_begin(slot)
        st = local_begin(jnp.minimum(ci + 1, nc - 1))
        scan_finish(ci, slot, s, ws)
        local_finish(st, 1 - slot)

    local_finish(local_begin(0), 0)

    def pair(j, carry):
        step(2 * j, 0)
        step(2 * j + 1, 1)
        return carry

    lax.fori_loop(0, nc // 2, pair, 0)


def _gdn(qkv, gcum, beta, gcum_t, nh):
    _, b, l, width = qkv.shape
    tl = min(1024, l)
    nt = l // tl
    nc = tl // CHUNK
    assert nc % 2 == 0
    li = lambda d, i: jnp.where(d == 0, i, nt - 1 - i)
    kern = functools.partial(_gdn_kernel, tl=tl, nh=nh)
    qkv_spec = lambda which: pl.BlockSpec((1, 1, tl, width), lambda bi, d, i: (which, bi, li(d, i), 0))
    return pl.pallas_call(
        kern,
        grid=(b, 2, nt),
        in_specs=[
            qkv_spec(0), qkv_spec(1), qkv_spec(2),
            pl.BlockSpec((1, 1, tl, nh), lambda bi, d, i: (bi, d, li(d, i), 0)),
            pl.BlockSpec((1, 1, tl, nh), lambda bi, d, i: (bi, d, li(d, i), 0)),
            pl.BlockSpec((1, 1, nc, nh, CHUNK), lambda bi, d, i: (bi, d, li(d, i), 0, 0)),
        ],
        out_specs=pl.BlockSpec((1, 1, tl, width), lambda bi, d, i: (d, bi, li(d, i), 0)),
        out_shape=jax.ShapeDtypeStruct((2, b, l, width), BF16),
        scratch_shapes=[
            pltpu.VMEM((nh, HEAD_DIM, HEAD_DIM), F32),
            pltpu.VMEM((2, nh, CHUNK, HEAD_DIM), F32),
            pltpu.VMEM((2, nh, 2 * CHUNK, HEAD_DIM), BF16),
            pltpu.VMEM((2, nh, CHUNK, HEAD_DIM), BF16),
            pltpu.VMEM((2, nh, CHUNK, CHUNK), BF16),
        ],
        compiler_params=_cparams(("parallel", "parallel", "arbitrary"), 40),
        name="gdn",
    )(qkv, qkv, qkv, gcum, beta, gcum_t)


def _mix_kernel(of_ref, ob_ref, z_ref, gm_ref, gp_ref, gn_ref, ga_ref, gb_ref,
                hn_ref, cw_ref, cb_ref, lg_ref, lb_ref, wa_ref, wb_ref,
                m_ref, scr, act, *, tl, rb, ntaps, wb_width):
    i = pl.program_id(1)
    last = pl.num_programs(1) - 1

    o = of_ref[0, 0].astype(F32) + ob_ref[0, 0].astype(F32)
    width = o.shape[1]
    nh = width // HEAD_DIM
    hn = hn_ref[...]
    parts = []
    for h in range(nh):
        oh = o[:, h * HEAD_DIM:(h + 1) * HEAD_DIM]
        ms = jnp.mean(oh * oh, axis=-1, keepdims=True)
        parts.append(oh * lax.rsqrt(ms + EPS) * hn)
    on = jnp.concatenate(parts, axis=-1)
    ya_in = (on * _silu(z_ref[0].astype(F32))).astype(BF16)
    y_a = _dot(ya_in, wa_ref[...])

    def glu(blk):
        blk = blk.astype(F32)
        return blk[:, :wb_width] * _sigmoid(blk[:, wb_width:])

    _conv_fill(scr, glu(gp_ref[0]), glu(gm_ref[0]), glu(gn_ref[0]), i, last, tl, ntaps)
    cw = cw_ref[...]
    cb = cb_ref[...]
    lg = lg_ref[...]
    lb = lb_ref[...]

    def body(r, carry):
        r0 = pl.multiple_of(r * rb, rb)
        acc = _conv_rows(scr, cw, r0, rb, ntaps) + cb
        mu = jnp.mean(acc, axis=-1, keepdims=True)
        xc = acc - mu
        var = jnp.mean(xc * xc, axis=-1, keepdims=True)
        y = xc * lax.rsqrt(var + EPS) * lg + lb
        act[pl.ds(r0, rb), :] = _silu(y).astype(BF16)
        return carry

    lax.fori_loop(0, tl // rb, body, 0)
    y_b = _dot(act[...], wb_ref[...])

    g_a = _sigmoid(ga_ref[0].astype(F32))
    g_b = _sigmoid(gb_ref[0].astype(F32))
    m_ref[0] = (g_a * y_a + g_b * y_b).astype(BF16)


def _mix(o2, p_big, head_norm, conv_w, conv_b, ln_g, ln_b, w_a, w_b, width, wb_width, d):
    _, b, l, _ = o2.shape
    tl = min(256, l)
    rb = 32
    hr = BF16_ROWS
    nhb = l // hr
    ntaps = conv_w.shape[0]
    z_blk = 3
    glu_blk = (4 * width) // (2 * wb_width)
    gate_blk = (4 * width + 2 * wb_width) // d
    assert (4 * width) % (2 * wb_width) == 0 and (4 * width + 2 * wb_width) % d == 0
    kern = functools.partial(_mix_kernel, tl=tl, rb=rb, ntaps=ntaps, wb_width=wb_width)
    const = lambda shape: pl.BlockSpec(shape, lambda bi, i: (0, 0))
    return pl.pallas_call(
        kern,
        grid=(b, l // tl),
        in_specs=[
            pl.BlockSpec((1, 1, tl, width), lambda bi, i: (0, bi, i, 0)),
            pl.BlockSpec((1, 1, tl, width), lambda bi, i: (1, bi, i, 0)),
            pl.BlockSpec((1, tl, width), lambda bi, i: (bi, i, z_blk)),
            pl.BlockSpec((1, tl, 2 * wb_width), lambda bi, i: (bi, i, glu_blk)),
            pl.BlockSpec((1, hr, 2 * wb_width),
                         lambda bi, i: (bi, jnp.maximum(i * (tl // hr) - 1, 0), glu_blk)),
            pl.BlockSpec((1, hr, 2 * wb_width),
                         lambda bi, i: (bi, jnp.minimum((i + 1) * (tl // hr), nhb - 1), glu_blk)),
            pl.BlockSpec((1, tl, d), lambda bi, i: (bi, i, gate_blk)),
            pl.BlockSpec((1, tl, d), lambda bi, i: (bi, i, gate_blk + 1)),
            const((1, HEAD_DIM)),
            const((ntaps, wb_width)), const((1, wb_width)), const((1, wb_width)), const((1, wb_width)),
            const((width, d)), const((wb_width, d)),
        ],
        out_specs=pl.BlockSpec((1, tl, d), lambda bi, i: (bi, i, 0)),
        out_shape=jax.ShapeDtypeStruct((b, l, d), BF16),
        scratch_shapes=[pltpu.VMEM((len(_conv_shifts(ntaps)), tl + 2 * hr, wb_width), F32),
                        pltpu.VMEM((tl, wb_width), BF16)],
        compiler_params=_cparams(("parallel", "parallel"), 48),
        name="mix",
    )(o2, o2, p_big, p_big, p_big, p_big, p_big, p_big,
      head_norm, conv_w, conv_b, ln_g, ln_b, w_a, w_b)


def _outproj_kernel(m_ref, x_ref, wo_ref, nf_ref, wr_ref, x1_ref, h2_ref, aff_ref):
    x1 = x_ref[0] + _dot(m_ref[0], wo_ref[...])
    x1_ref[0] = x1
    ms = jnp.mean(x1 * x1, axis=-1, keepdims=True)
    h2 = x1 * lax.rsqrt(ms + EPS) * nf_ref[...]
    h2_ref[0] = h2
    logits = _dot_hi(h2, wr_ref[...])
    mx = jnp.max(logits, axis=-1, keepdims=True)
    ex = jnp.exp(logits - mx)
    aff_ref[0] = ex / jnp.sum(ex, axis=-1, keepdims=True)


def _outproj(merged, x, w_out, norm_ffn, w_router):
    b, l, d = x.shape
    ne = w_router.shape[1]
    tl = min(256, l)
    const = lambda shape: pl.BlockSpec(shape, lambda bi, i: (0, 0))
    tok = lambda last: pl.BlockSpec((1, tl, last), lambda bi, i: (bi, i, 0))
    return pl.pallas_call(
        _outproj_kernel,
        grid=(b, l // tl),
        in_specs=[tok(d), tok(d), const((d, d)), const((1, d)), const((d, ne))],
        out_specs=[tok(d), tok(d), tok(ne)],
        out_shape=[
            jax.ShapeDtypeStruct((b, l, d), F32),
            jax.ShapeDtypeStruct((b, l, d), F32),
            jax.ShapeDtypeStruct((b, l, ne), F32),
        ],
        compiler_params=_cparams(("parallel", "parallel"), 48),
        name="outproj",
    )(merged, x, w_out, norm_ffn, w_router)


def _route_kernel(a_ref, idx_ref, pos_ref, rs_ref, *, cap, nrows):
    a = a_ref[0]
    bits = lax.bitcast_convert_type(a, I32)
    capf = jnp.float32(cap)

    def count(mask):
        return jnp.sum(jnp.where(mask, 1.0, 0.0), keepdims=True)

    thr = jnp.zeros((1, 1), I32)
    for bit in range(30, -1, -1):
        cand = thr | (1 << bit)
        thr = jnp.where(count(bits >= cand) >= capf, cand, thr)

    upper = _bf01(_iota((LANES, LANES), 0) <= _iota((LANES, LANES), 1))
    ones_ll = _ones_bf16((LANES, LANES))
    strict_lower = _bf01(_iota((nrows, nrows), 1) < _iota((nrows, nrows), 0))

    def prefix(m01):
        within = _dot(m01, upper)
        tot = _dot(m01, ones_ll)
        before = _dot(strict_lower, tot.astype(BF16))
        return within + before

    gt = bits > thr
    eq = bits == thr
    need = capf - count(gt)
    eq_rank = prefix(_bf01(eq))
    mask = gt | (eq & (eq_rank <= need))
    m01 = _bf01(mask)
    incl = prefix(m01)
    pos_ref[0] = jnp.where(mask, incl - 1.0, -1.0).astype(I32)

    ones_8l = _ones_bf16((8, LANES))
    tot_row = _dot_nt(ones_8l, m01)
    strict_upper = _bf01(_iota((nrows, nrows), 0) < _iota((nrows, nrows), 1))
    start_row = _dot(tot_row.astype(BF16), strict_upper)
    end_row = start_row + tot_row
    rs_ref[0] = start_row[0:1, :].astype(I32)

    s_col = _iota((cap, nrows), 0).astype(F32)
    g01 = _bf01((start_row[0:1, :] <= s_col) & (s_col < end_row[0:1, :]))
    hi = jnp.floor(incl * (1.0 / 256.0))
    lo = incl - 256.0 * hi
    rank_at = 256.0 * _dot(g01, hi.astype(BF16)) + _dot(g01, lo.astype(BF16))
    s_lane = _iota((cap, LANES), 0).astype(F32)
    ind = _bf01(rank_at <= s_lane)
    local_row = _dot_nt(ones_8l, ind)
    rvals = _iota((8, nrows), 1).astype(F32).astype(BF16)
    row_of = _dot_nt(rvals, g01)
    idx_ref[0] = (row_of[0:1, :] * float(LANES) + local_row[0:1, :]).astype(I32)


def _route(aff_t3, cap):
    ne, nrows, _ = aff_t3.shape
    assert nrows <= 256 and cap % LANES == 0
    kern = functools.partial(_route_kernel, cap=cap, nrows=nrows)
    return pl.pallas_call(
        kern,
        grid=(ne,),
        in_specs=[pl.BlockSpec((1, nrows, LANES), lambda e: (e, 0, 0))],
        out_specs=[
            pl.BlockSpec((1, 1, cap), lambda e: (e, 0, 0)),
            pl.BlockSpec((1, nrows, LANES), lambda e: (e, 0, 0)),
            pl.BlockSpec((1, 1, nrows), lambda e: (e, 0, 0)),
        ],
        out_shape=[
            jax.ShapeDtypeStruct((ne, 1, cap), I32),
            jax.ShapeDtypeStruct((ne, nrows, LANES), I32),
            jax.ShapeDtypeStruct((ne, 1, nrows), I32),
        ],
        compiler_params=_cparams(("parallel",), 48),
        name="route",
    )(aff_t3)


def _expert_kernel(idx_ref, h2_hbm, wg_ref, wu_ref, wd_ref, ye_ref, buf, sem, *, ts, nreal, nfc):
    step = pl.program_id(0)
    slot = step % 2

    def row_copy(st, sl, r):
        t = idx_ref[st * ts + r]
        return pltpu.make_async_copy(h2_hbm.at[pl.ds(t, 1), :], buf.at[sl, pl.ds(r, 1), :], sem.at[sl])

    def wait_tile(sl):
        pltpu.make_async_copy(h2_hbm.at[pl.ds(0, ts), :], buf.at[sl], sem.at[sl]).wait()

    @pl.when(step == 0)
    def _():
        def body(r, carry):
            row_copy(0, 0, r).start()
            return carry
        lax.fori_loop(0, ts, body, 0)

    @pl.when(step < nreal)
    def _():
        wait_tile(slot)
        x = buf[slot].astype(BF16)
        nxt = jnp.minimum(step + 1, nreal - 1)
        rows_per = ts // nfc
        fc = wg_ref.shape[2] // nfc
        y = None
        for c in range(nfc):
            for r in range(c * rows_per, (c + 1) * rows_per):
                row_copy(nxt, 1 - slot, r).start()
            a = _dot(x, wg_ref[0, :, c * fc:(c + 1) * fc])
            b = _dot(x, wu_ref[0, :, c * fc:(c + 1) * fc])
            hm = (_silu(a) * b).astype(BF16)
            t = _dot(hm, wd_ref[0, c * fc:(c + 1) * fc, :])
            y = t if y is None else y + t
        ye_ref[...] = y.astype(BF16)

        @pl.when(step == nreal - 1)
        def _():
            wait_tile(1 - slot)

    @pl.when(step >= nreal)
    def _():
        ye_ref[...] = jnp.zeros_like(ye_ref)


def _experts(idx_flat, h2, w_gate, w_up, w_down, cap):
    n, d = h2.shape
    ne, _, f = w_gate.shape
    ts = min(256, cap)
    nj = cap // ts
    nreal = ne * nj
    nfc = 4
    assert ts % nfc == 0 and f % nfc == 0
    kern = functools.partial(_expert_kernel, ts=ts, nreal=nreal, nfc=nfc)
    w_idx = lambda s, idx: (jnp.minimum(s // nj, ne - 1), 0, 0)
    grid_spec = pltpu.PrefetchScalarGridSpec(
        num_scalar_prefetch=1,
        grid=(nreal + 1,),
        in_specs=[
            pl.BlockSpec(memory_space=pl.ANY),
            pl.BlockSpec((1, d, f), w_idx),
            pl.BlockSpec((1, d, f), w_idx),
            pl.BlockSpec((1, f, d), w_idx),
        ],
        out_specs=pl.BlockSpec((ts, d), lambda s, idx: (s, 0)),
        scratch_shapes=[pltpu.VMEM((2, ts, d), F32), pltpu.SemaphoreType.DMA((2,))],
    )
    return pl.pallas_call(
        kern,
        grid_spec=grid_spec,
        out_shape=jax.ShapeDtypeStruct((ne * cap + ts, d), BF16),
        compiler_params=_cparams(("arbitrary",), 56),
        name="experts",
    )(idx_flat, h2, w_gate, w_up, w_down)


def _combine_kernel(ts_ref, x1_ref, pos_ref, aff_ref, nf_ref, ye_hbm, y_ref, buf, sem,
                    *, tt, cap, ne, piece):
    i = pl.program_id(0)
    nsteps = pl.num_programs(0)
    slot = i % 2
    br = tt + BF16_ROWS
    pieces = [(0, piece), (piece, piece), (2 * piece, br - 2 * piece)]
    assert br > 2 * piece

    def start_of(st, e):
        first = e * cap + ts_ref[st * ne + e]
        return pl.multiple_of(jnp.bitwise_and(first, -BF16_ROWS), BF16_ROWS)

    def rows_needed(st, e):
        first = e * cap + ts_ref[st * ne + e]
        return first - start_of(st, e) + ts_ref[(st + 1) * ne + e] - ts_ref[st * ne + e]

    def copy(st, sl, e, p):
        off, size = pieces[p]
        return pltpu.make_async_copy(ye_hbm.at[pl.ds(start_of(st, e) + off, size), :],
                                     buf.at[sl, pl.ds(e * br + off, size), :], sem.at[sl])

    def for_each_copy(st, sl, fn):
        for e in range(ne):
            fn(copy(st, sl, e, 0))
        for e in range(ne):
            for p in range(1, len(pieces)):
                @pl.when(rows_needed(st, e) > pieces[p][0])
                def _():
                    fn(copy(st, sl, e, p))

    @pl.when(i == 0)
    def _():
        buf[...] = jnp.zeros_like(buf)
        for_each_copy(0, 0, lambda c: c.start())

    @pl.when(i + 1 < nsteps)
    def _():
        for_each_copy(i + 1, 1 - slot, lambda c: c.start())

    for_each_copy(i, slot, lambda c: c.wait())

    pos = pos_ref[...]
    aff = aff_ref[...]
    lane_e = _iota((1, ne), 1)
    base = jnp.zeros((1, ne), I32)
    for e in range(ne):
        base = jnp.where(lane_e == e, start_of(i, e) - e * cap, base)
    rel = jnp.clip(pos - base, -1, br).astype(F32)
    kdim = ne * br
    c_idx = _iota((ne, kdim), 1)
    e_idx = _iota((ne, kdim), 0)
    expand = _bf01((c_idx >= e_idx * br) & (c_idx < (e_idx + 1) * br))
    e_of_c = jnp.sum(jnp.where(c_idx >= (e_idx + 1) * br, 1.0, 0.0), axis=0, keepdims=True)
    k_of_c = _iota((1, kdim), 1).astype(F32) - float(br) * e_of_c
    rel_x = _dot(rel.astype(BF16), expand)
    aff_x = _dot(aff.astype(BF16), expand)
    emat = jnp.where(rel_x == k_of_c, aff_x, 0.0).astype(BF16)
    out = x1_ref[...] + _dot(emat, buf[slot])
    ms = jnp.mean(out * out, axis=-1, keepdims=True)
    y_ref[...] = out * lax.rsqrt(ms + EPS) * nf_ref[...]


def _combine(tile_starts, x1, pos_tm, aff, norm_final, ye, cap):
    n, d = x1.shape
    ne = aff.shape[1]
    tt = LANES
    br = tt + BF16_ROWS
    assert cap % BF16_ROWS == 0 and ye.shape[0] >= ne * cap + br
    kern = functools.partial(_combine_kernel, tt=tt, cap=cap, ne=ne, piece=64)
    grid_spec = pltpu.PrefetchScalarGridSpec(
        num_scalar_prefetch=1,
        grid=(n // tt,),
        in_specs=[
            pl.BlockSpec((tt, d), lambda i, ts: (i, 0)),
            pl.BlockSpec((tt, ne), lambda i, ts: (i, 0)),
            pl.BlockSpec((tt, ne), lambda i, ts: (i, 0)),
            pl.BlockSpec((1, d), lambda i, ts: (0, 0)),
            pl.BlockSpec(memory_space=pl.ANY),
        ],
        out_specs=pl.BlockSpec((tt, d), lambda i, ts: (i, 0)),
        scratch_shapes=[pltpu.VMEM((2, ne * br, d), BF16), pltpu.SemaphoreType.DMA((2,))],
    )
    return pl.pallas_call(
        kern,
        grid_spec=grid_spec,
        out_shape=jax.ShapeDtypeStruct((n, d), F32),
        compiler_params=_cparams(("arbitrary",), 48),
        name="combine",
    )(tile_starts, x1, pos_tm, aff, norm_final, ye)


def _trunk(x, w, nh, width, wb_width):
    b, l, d = x.shape
    n = b * l
    ne = w["w_router"].shape[1]
    cap = max(1, CAPACITY_FACTOR * n // ne)

    p_big, ab = _inproj(x, w["norm_mix"], w["w_big"], w["w_small"], width)
    qkv = _prep(p_big, w["conv_a_w"], width)
    gcum, beta, gcum_t = _gates(ab, w["a_log"], w["dt_bias"], nh)
    o2 = _gdn(qkv, gcum, beta, gcum_t, nh)
    merged = _mix(o2, p_big, w["head_norm"], w["conv_b_w"], w["conv_b_b"], w["ln_b_g"], w["ln_b_b"],
                  w["w_proj_a"], w["w_proj_b"], width, wb_width, d)
    x1, h2, aff = _outproj(merged, x, w["w_out"], w["norm_ffn"], w["w_router"])

    aff = aff.reshape(n, ne)
    nrows = n // LANES
    idx, pos, row_start = _route(aff.T.reshape(ne, nrows, LANES), cap)
    idx_flat = idx.reshape(ne * cap)
    pos_tm = pos.reshape(ne, n).T
    tile_starts = jnp.concatenate(
        [row_start.reshape(ne, nrows).T, jnp.full((1, ne), cap, I32)], axis=0).reshape(-1)

    ye = _experts(idx_flat, h2.reshape(n, d), w["w_gate"], w["w_up"], w["w_down"], cap)
    y = _combine(tile_starts, x1.reshape(n, d), pos_tm, aff, w["norm_final"], ye, cap)
    return y.reshape(b, l, d)


def kernel(x_prompt, x_sample, norm_mix, w_in, conv_a_w, a_log, dt_bias, head_norm, w_proj_a,
           conv_b_w, conv_b_b, ln_b_g, ln_b_b, w_proj_b, w_out, norm_ffn, w_router, w_gate, w_up,
           w_down, norm_final):
    assert w_in.shape[0] == 1, "single layer"
    width = conv_a_w.shape[-1] // 3
    wb_width = conv_b_w.shape[-1]
    nh = width // HEAD_DIM
    off_small = 4 * width
    off_glu = off_small + 4 * nh
    w_in0 = w_in[0]
    row = lambda v: v.reshape(1, -1)
    w = {
        "norm_mix": row(norm_mix[0]),
        "w_big": jnp.concatenate([w_in0[:, :off_small], w_in0[:, off_glu:]], axis=1).astype(BF16),
        "w_small": w_in0[:, off_small:off_glu],
        "conv_a_w": conv_a_w[0],
        "a_log": a_log[0],
        "dt_bias": dt_bias[0],
        "head_norm": row(head_norm[0]),
        "w_proj_a": w_proj_a[0].astype(BF16),
        "conv_b_w": conv_b_w[0],
        "conv_b_b": row(conv_b_b[0]),
        "ln_b_g": row(ln_b_g[0]),
        "ln_b_b": row(ln_b_b[0]),
        "w_proj_b": w_proj_b[0].astype(BF16),
        "w_out": w_out[0].astype(BF16),
        "norm_ffn": row(norm_ffn[0]),
        "w_router": w_router[0],
        "w_gate": w_gate[0].astype(BF16),
        "w_up": w_up[0].astype(BF16),
        "w_down": w_down[0].astype(BF16),
        "norm_final": row(norm_final),
    }
    y_prompt = _trunk(x_prompt, w, nh, width, wb_width)
    y_sample = _trunk(x_sample, w, nh, width, wb_width)
    return (y_prompt, y_sample)
```

```python
import functools

import jax
import jax.numpy as jnp
from jax import lax
from jax.experimental import pallas as pl
from jax.experimental.pallas import tpu as pltpu

F32 = jnp.float32
BF16 = jnp.bfloat16
I32 = jnp.int32

HEAD_DIM = 128
CHUNK = 64
CHUNK_SHIFT = 6
CAPACITY_FACTOR = 2
EPS = 1e-6
LANES = 128
BF16_ROWS = 16
MIB = 1024 * 1024


def _cparams(sem, vmem_mib):
    return pltpu.CompilerParams(dimension_semantics=sem, vmem_limit_bytes=vmem_mib * MIB)


def _dot(a, b):
    return jnp.dot(a, b, preferred_element_type=F32)


def _dot_nt(a, b):
    return lax.dot_general(a, b, (((1,), (1,)), ((), ())), preferred_element_type=F32)


def _dot_tn(a, b):
    return lax.dot_general(a, b, (((0,), (0,)), ((), ())), preferred_element_type=F32)


def _split2(a):
    hi = a.astype(BF16)
    lo = (a - hi.astype(F32)).astype(BF16)
    return hi, lo


def _split3(a):
    hi = a.astype(BF16)
    r = a - hi.astype(F32)
    mid = r.astype(BF16)
    lo = (r - mid.astype(F32)).astype(BF16)
    return hi, mid, lo


def _dot_hi(a, b):
    ah, al = _split2(a)
    bh, bl = _split2(b)
    return _dot(ah, bh) + _dot(ah, bl) + _dot(al, bh)


def _sigmoid(x):
    return 0.5 * jnp.tanh(0.5 * x) + 0.5


def _silu(x):
    return x * _sigmoid(x)


def _softplus(x):
    return jnp.maximum(x, 0.0) + jnp.log(1.0 + jnp.exp(-jnp.abs(x)))


def _iota(shape, dim):
    return lax.broadcasted_iota(I32, shape, dim)


def _ones_bf16(shape):
    return jnp.ones(shape, BF16)


def _bf01(mask):
    return jnp.where(mask, 1.0, 0.0).astype(BF16)


def _inproj_kernel(x_ref, g_ref, wb_ref, ws_ref, p_ref, ab_ref, h_scr):
    @pl.when(pl.program_id(2) == 0)
    def _():
        rows = min(256, h_scr.shape[0])
        for r in range(h_scr.shape[0] // rows):
            x = x_ref[0, r * rows:(r + 1) * rows, :]
            ms = jnp.mean(x * x, axis=-1, keepdims=True)
            h = x * lax.rsqrt(ms + EPS) * g_ref[...]
            h_scr[r * rows:(r + 1) * rows, :] = h.astype(BF16)
            ab_ref[0, r * rows:(r + 1) * rows, :] = _dot_hi(h, ws_ref[...])

    p_ref[0] = _dot(h_scr[...], wb_ref[...]).astype(BF16)


def _inproj(x, g, w_big, w_small, tn):
    b, l, d = x.shape
    nbig = w_big.shape[1]
    nsmall = w_small.shape[1]
    tm = min(1024, l)
    return pl.pallas_call(
        _inproj_kernel,
        grid=(b, l // tm, nbig // tn),
        in_specs=[
            pl.BlockSpec((1, tm, d), lambda bi, i, j: (bi, i, 0)),
            pl.BlockSpec((1, d), lambda bi, i, j: (0, 0)),
            pl.BlockSpec((d, tn), lambda bi, i, j: (0, j)),
            pl.BlockSpec((d, nsmall), lambda bi, i, j: (0, 0)),
        ],
        out_specs=[
            pl.BlockSpec((1, tm, tn), lambda bi, i, j: (bi, i, j)),
            pl.BlockSpec((1, tm, nsmall), lambda bi, i, j: (bi, i, 0)),
        ],
        out_shape=[
            jax.ShapeDtypeStruct((b, l, nbig), BF16),
            jax.ShapeDtypeStruct((b, l, nsmall), F32),
        ],
        scratch_shapes=[pltpu.VMEM((tm, d), BF16)],
        compiler_params=_cparams(("parallel", "parallel", "arbitrary"), 48),
        name="inproj",
    )(x, g, w_big, w_small)


SUBLANES = 8


def _conv_shifts(ntaps):
    first = BF16_ROWS - ntaps // 2
    return sorted({(first + t) % SUBLANES for t in range(ntaps)})


def _conv_fill(scr, prev, main, nxt, i, last, tl, ntaps):
    hr = BF16_ROWS
    shifts = _conv_shifts(ntaps)
    rows = tl + 2 * hr
    base = shifts.index(0)
    scr[base, 0:hr, :] = jnp.where(i > 0, prev, 0.0)
    scr[base, hr:hr + tl, :] = main
    scr[base, hr + tl:rows, :] = jnp.where(i < last, nxt, 0.0)
    full = scr[base]
    for c, s in enumerate(shifts):
        if s:
            scr[c] = pltpu.roll(full, rows - s, axis=0)


def _conv_rows(scr, w, r0, rb, ntaps):
    shifts = _conv_shifts(ntaps)
    first = BF16_ROWS - ntaps // 2
    acc = None
    for t in range(ntaps):
        off = first + t
        s = off % SUBLANES
        win = scr[shifts.index(s), pl.ds(r0 + (off - s), rb), :]
        term = win * w[t:t + 1, :]
        acc = term if acc is None else acc + term
    return acc


def _prep_kernel(pm_ref, pp_ref, pn_ref, cw_ref, o_ref, scr, *, tl, rb, ntaps):
    i = pl.program_id(1)
    j = pl.program_id(2)
    last = pl.num_programs(1) - 1
    _conv_fill(scr, pp_ref[0].astype(F32), pm_ref[0].astype(F32), pn_ref[0].astype(F32),
               i, last, tl, ntaps)
    w = cw_ref[...]
    width = w.shape[1]
    nh = width // HEAD_DIM
    qscale = jnp.where(j == 0, HEAD_DIM ** -0.5, 1.0).astype(F32)
    is_qk = j < 2

    def body(r, carry):
        r0 = pl.multiple_of(r * rb, rb)
        y = _silu(_conv_rows(scr, w, r0, rb, ntaps))
        outs = []
        for h in range(nh):
            yh = y[:, h * HEAD_DIM:(h + 1) * HEAD_DIM]
            ss = jnp.sum(yh * yh, axis=-1, keepdims=True)
            fac = jnp.where(is_qk, lax.rsqrt(ss + EPS) * qscale, 1.0)
            outs.append(yh * fac)
        o_ref[0, 0, pl.ds(r0, rb), :] = jnp.concatenate(outs, axis=-1).astype(BF16)
        return carry

    lax.fori_loop(0, tl // rb, body, 0)


def _prep(p_big, conv_w, width):
    b, l, _ = p_big.shape
    tl = min(256, l)
    rb = 32
    hr = BF16_ROWS
    nhb = l // hr
    ntaps = conv_w.shape[0]
    assert ntaps // 2 < hr
    kern = functools.partial(_prep_kernel, tl=tl, rb=rb, ntaps=ntaps)
    return pl.pallas_call(
        kern,
        grid=(b, l // tl, 3),
        in_specs=[
            pl.BlockSpec((1, tl, width), lambda bi, i, j: (bi, i, j)),
            pl.BlockSpec((1, hr, width), lambda bi, i, j: (bi, jnp.maximum(i * (tl // hr) - 1, 0), j)),
            pl.BlockSpec((1, hr, width), lambda bi, i, j: (bi, jnp.minimum((i + 1) * (tl // hr), nhb - 1), j)),
            pl.BlockSpec((ntaps, width), lambda bi, i, j: (0, j)),
        ],
        out_specs=pl.BlockSpec((1, 1, tl, width), lambda bi, i, j: (j, bi, i, 0)),
        out_shape=jax.ShapeDtypeStruct((3, b, l, width), BF16),
        scratch_shapes=[pltpu.VMEM((len(_conv_shifts(ntaps)), tl + 2 * hr, width), F32)],
        compiler_params=_cparams(("parallel", "parallel", "arbitrary"), 32),
        name="prep",
    )(p_big, p_big, p_big, conv_w)


def _gates_kernel(ab_ref, at_ref, arow_ref, dtrow_ref, acol_ref, dtcol_ref,
                  gc_ref, bt_ref, gct_ref, *, tl, nh):
    ab = ab_ref[0]
    beta = _sigmoid(ab[:, :2 * nh])
    alpha = ab[:, 2 * nh:]
    g = -jnp.exp(arow_ref[...]) * _softplus(alpha + dtrow_ref[...])
    r = _iota((tl, tl), 0)
    c = _iota((tl, tl), 1)
    same = jnp.right_shift(r, CHUNK_SHIFT) == jnp.right_shift(c, CHUNK_SHIFT)
    m_f = _bf01(same & (c <= r))
    m_b = _bf01(same & (c >= r))
    g3 = _split3(g)
    gf = _dot(m_f, g3[0]) + _dot(m_f, g3[1]) + _dot(m_f, g3[2])
    gb = _dot(m_b, g3[0]) + _dot(m_b, g3[1]) + _dot(m_b, g3[2])
    gc_ref[0, 0] = gf[:, :nh]
    gc_ref[0, 1] = gb[:, nh:]
    bt_ref[0, 0] = beta[:, :nh]
    bt_ref[0, 1] = beta[:, nh:]
    nc = tl // CHUNK
    at = at_ref[0]
    gt = -jnp.exp(acol_ref[...])[None] * _softplus(at + dtcol_ref[...][None])
    gt2 = gt.reshape(nc * 2 * nh, CHUNK)
    rr = _iota((CHUNK, CHUNK), 0)
    cc = _iota((CHUNK, CHUNK), 1)
    u_f = _bf01(rr <= cc)
    u_b = _bf01(rr >= cc)
    t3 = _split3(gt2)
    cf = (_dot(t3[0], u_f) + _dot(t3[1], u_f) + _dot(t3[2], u_f)).reshape(nc, 2 * nh, CHUNK)
    cb = (_dot(t3[0], u_b) + _dot(t3[1], u_b) + _dot(t3[2], u_b)).reshape(nc, 2 * nh, CHUNK)
    gct_ref[0, 0] = cf[:, :nh, :]
    gct_ref[0, 1] = cb[:, nh:, :]


def _gates(ab, a_log, dt_bias, nh):
    b, l, _ = ab.shape
    tl = min(512, l)
    nc = tl // CHUNK
    at = ab[..., 2 * nh:].reshape(b, l // CHUNK, CHUNK, 2 * nh).swapaxes(-1, -2)
    arow = a_log.reshape(1, 2 * nh)
    dtrow = dt_bias.reshape(1, 2 * nh)
    acol = a_log.reshape(2 * nh, 1)
    dtcol = dt_bias.reshape(2 * nh, 1)
    kern = functools.partial(_gates_kernel, tl=tl, nh=nh)
    small = lambda shape: pl.BlockSpec(shape, lambda bi, i: (0, 0))
    return pl.pallas_call(
        kern,
        grid=(b, l // tl),
        in_specs=[
            pl.BlockSpec((1, tl, 4 * nh), lambda bi, i: (bi, i, 0)),
            pl.BlockSpec((1, nc, 2 * nh, CHUNK), lambda bi, i: (bi, i, 0, 0)),
            small((1, 2 * nh)), small((1, 2 * nh)), small((2 * nh, 1)), small((2 * nh, 1)),
        ],
        out_specs=[
            pl.BlockSpec((1, 2, tl, nh), lambda bi, i: (bi, 0, i, 0)),
            pl.BlockSpec((1, 2, tl, nh), lambda bi, i: (bi, 0, i, 0)),
            pl.BlockSpec((1, 2, nc, nh, CHUNK), lambda bi, i: (bi, 0, i, 0, 0)),
        ],
        out_shape=[
            jax.ShapeDtypeStruct((b, 2, l, nh), F32),
            jax.ShapeDtypeStruct((b, 2, l, nh), F32),
            jax.ShapeDtypeStruct((b, 2, l // CHUNK, nh, CHUNK), F32),
        ],
        compiler_params=_cparams(("parallel", "parallel"), 32),
        name="gates",
    )(ab, at, arow, dtrow, acol, dtcol)


def _gdn2_kernel(qf_ref, kf_ref, vf_ref, qb_ref, kb_ref, vb_ref,
                 gcf_ref, btf_ref, gctf_ref, gcb_ref, btb_ref, gctb_ref,
                 of_ref, ob_ref, s_scr, u_scr, wq_scr, ke_scr, qk_scr, *, tl, nh):
    i = pl.program_id(1)

    @pl.when(i == 0)
    def _():
        s_scr[...] = jnp.zeros_like(s_scr)

    nc = tl // CHUNK
    row = _iota((CHUNK, CHUNK), 0)
    col = _iota((CHUNK, CHUNK), 1)
    m_incl = [row >= col, row <= col]
    m_strict = [row > col, row < col]
    eye = jnp.where(row == col, 1.0, 0.0).astype(F32)
    refs = [dict(q=qf_ref, k=kf_ref, v=vf_ref, gc=gcf_ref, bt=btf_ref, gct=gctf_ref, o=of_ref),
            dict(q=qb_ref, k=kb_ref, v=vb_ref, gc=gcb_ref, bt=btb_ref, gct=gctb_ref, o=ob_ref)]
    streams = range(2 * nh)
    dr = [s // nh for s in streams]
    hd = [s % nh for s in streams]
    sl = [slice(hd[s] * HEAD_DIM, (hd[s] + 1) * HEAD_DIM) for s in streams]

    def chunk_rows(d, ci):
        c = ci if d == 0 else nc - 1 - ci
        return c, pl.multiple_of(c * CHUNK, CHUNK)

    def last_row(d, gcols):
        return gcols[CHUNK - 1:CHUNK, :] if d == 0 else gcols[0:1, :]

    def local_begin(ci):
        cr = [chunk_rows(d, ci) for d in range(2)]
        gcols = [refs[d]["gc"][0, 0, pl.ds(cr[d][1], CHUNK), :] for d in range(2)]
        bcols = [refs[d]["bt"][0, 0, pl.ds(cr[d][1], CHUNK), :] for d in range(2)]
        grows = [refs[d]["gct"][0, 0, cr[d][0]] for d in range(2)]
        glast = [last_row(d, gcols[d]) for d in range(2)]
        q = [refs[dr[s]]["q"][0, 0, pl.ds(cr[dr[s]][1], CHUNK), sl[s]] for s in streams]
        k16 = [refs[dr[s]]["k"][0, 0, pl.ds(cr[dr[s]][1], CHUNK), sl[s]] for s in streams]
        k = [k16[s].astype(F32) for s in streams]
        v = [refs[dr[s]]["v"][0, 0, pl.ds(cr[dr[s]][1], CHUNK), sl[s]].astype(F32) for s in streams]
        gcol = [gcols[dr[s]][:, hd[s]:hd[s] + 1] for s in streams]
        beta = [bcols[dr[s]][:, hd[s]:hd[s] + 1] for s in streams]
        grow = [grows[dr[s]][hd[s]:hd[s] + 1, :] for s in streams]
        gl = [glast[dr[s]][:, hd[s]:hd[s] + 1] for s in streams]
        kb = [k[s] * beta[s] for s in streams]
        kk = [_dot_nt(kb[s].astype(BF16), k16[s]) for s in streams]
        qk = [_dot_nt(q[s], k16[s]) for s in streams]
        return dict(grow=grow, q=q, k=k, v=v, gcol=gcol, beta=beta, gl=gl, kb=kb, kk=kk, qk=qk)

    def local_finish(st, slot):
        gcol, beta, kb = st["gcol"], st["beta"], st["kb"]
        dec = [jnp.where(m_incl[dr[s]],
                         jnp.exp(jnp.where(m_incl[dr[s]], gcol[s] - st["grow"][s], 0.0)), 0.0) for s in streams]
        pw = [jnp.where(m_strict[dr[s]], -(st["kk"][s] * dec[s]), 0.0) for s in streams]
        p16 = [pw[s].astype(BF16) for s in streams]
        tm = [eye + pw[s] for s in streams]
        for _ in range(5):
            pw = [_dot(p16[s], p16[s]) for s in streams]
            p16 = [pw[s].astype(BF16) for s in streams]
            tm = [tm[s] + _dot(tm[s].astype(BF16), p16[s]) for s in streams]
        eg = [jnp.exp(gcol[s]) for s in streams]
        rhs = [jnp.concatenate([st["v"][s] * beta[s], kb[s] * eg[s]], axis=-1).astype(BF16) for s in streams]
        sol = [_dot(tm[s].astype(BF16), rhs[s]) for s in streams]
        for s in streams:
            u_scr[slot, s] = sol[s][:, :HEAD_DIM]
            wq_scr[slot, s, 0:CHUNK, :] = sol[s][:, HEAD_DIM:].astype(BF16)
            wq_scr[slot, s, CHUNK:2 * CHUNK, :] = (st["q"][s].astype(F32) * eg[s]).astype(BF16)
            ke_scr[slot, s] = (st["k"][s] * jnp.exp(st["gl"][s] - gcol[s])).astype(BF16)
            qk_scr[slot, s] = (st["qk"][s] * dec[s]).astype(BF16)

    def scan_begin(slot):
        state = [s_scr[s] for s in streams]
        ws = [_dot(wq_scr[slot, s], state[s].astype(BF16)) for s in streams]
        return state, ws

    def scan_finish(ci, slot, state, ws):
        cr = [chunk_rows(d, ci) for d in range(2)]
        glast = [last_row(d, refs[d]["gc"][0, 0, pl.ds(cr[d][1], CHUNK), :]) for d in range(2)]
        vn16 = [(u_scr[slot, s] - ws[s][:CHUNK]).astype(BF16) for s in streams]
        s_add = [_dot_tn(ke_scr[slot, s], vn16[s]) for s in streams]
        o_in = [_dot(qk_scr[slot, s], vn16[s]) for s in streams]
        for s in streams:
            s_scr[s] = state[s] * jnp.exp(glast[dr[s]][:, hd[s]:hd[s] + 1]) + s_add[s]
            refs[dr[s]]["o"][0, pl.ds(cr[dr[s]][1], CHUNK), sl[s]] = (ws[s][CHUNK:] + o_in[s]).astype(BF16)

    def step(ci, slot):
        state, ws = scan_begin(slot)
        st = local_begin(jnp.minimum(ci + 1, nc - 1))
        scan_finish(ci, slot, state, ws)
        local_finish(st, 1 - slot)

    local_finish(local_begin(0), 0)

    def pair(j, carry):
        step(2 * j, 0)
        step(2 * j + 1, 1)
        return carry

    lax.fori_loop(0, nc // 2, pair, 0)


def _gdn2(qkv, gcum, beta, gcum_t, nh):
    _, b, l, width = qkv.shape
    tl = min(512, l)
    nt = l // tl
    nc = tl // CHUNK
    assert nc % 2 == 0
    kern = functools.partial(_gdn2_kernel, tl=tl, nh=nh)
    fw = lambda i: i
    bw = lambda i: nt - 1 - i
    qkv_spec = lambda which, t: pl.BlockSpec((1, 1, tl, width), lambda bi, i: (which, bi, t(i), 0))
    col_spec = lambda d, t: pl.BlockSpec((1, 1, tl, nh), lambda bi, i: (bi, d, t(i), 0))
    row_spec = lambda d, t: pl.BlockSpec((1, 1, nc, nh, CHUNK), lambda bi, i: (bi, d, t(i), 0, 0))
    ns = 2 * nh
    return pl.pallas_call(
        kern,
        grid=(b, nt),
        in_specs=[
            qkv_spec(0, fw), qkv_spec(1, fw), qkv_spec(2, fw),
            qkv_spec(0, bw), qkv_spec(1, bw), qkv_spec(2, bw),
            col_spec(0, fw), col_spec(0, fw), row_spec(0, fw),
            col_spec(1, bw), col_spec(1, bw), row_spec(1, bw),
        ],
        out_specs=[pl.BlockSpec((1, tl, width), lambda bi, i: (bi, i, 0)),
                   pl.BlockSpec((1, tl, width), lambda bi, i: (bi, nt - 1 - i, 0))],
        out_shape=[jax.ShapeDtypeStruct((b, l, width), BF16)] * 2,
        scratch_shapes=[
            pltpu.VMEM((ns, HEAD_DIM, HEAD_DIM), F32),
            pltpu.VMEM((2, ns, CHUNK, HEAD_DIM), F32),
            pltpu.VMEM((2, ns, 2 * CHUNK, HEAD_DIM), BF16),
            pltpu.VMEM((2, ns, CHUNK, HEAD_DIM), BF16),
            pltpu.VMEM((2, ns, CHUNK, CHUNK), BF16),
        ],
        compiler_params=_cparams(("parallel", "arbitrary"), 48),
        name="gdn",
    )(qkv, qkv, qkv, qkv, qkv, qkv, gcum, beta, gcum_t, gcum, beta, gcum_t)


def _mix_kernel(of_ref, ob_ref, z_ref, gm_ref, gp_ref, gn_ref, ga_ref, gb_ref,
                hn_ref, cw_ref, cb_ref, lg_ref, lb_ref, wa_ref, wb_ref,
                m_ref, scr, act, *, tl, rb, ntaps, wb_width):
    i = pl.program_id(1)
    last = pl.num_programs(1) - 1

    o = of_ref[0].astype(F32) + ob_ref[0].astype(F32)
    width = o.shape[1]
    nh = width // HEAD_DIM
    hn = hn_ref[...]
    parts = []
    for h in range(nh):
        oh = o[:, h * HEAD_DIM:(h + 1) * HEAD_DIM]
        ms = jnp.mean(oh * oh, axis=-1, keepdims=True)
        parts.append(oh * lax.rsqrt(ms + EPS) * hn)
    on = jnp.concatenate(parts, axis=-1)
    ya_in = (on * _silu(z_ref[0].astype(F32))).astype(BF16)
    y_a = _dot(ya_in, wa_ref[...])

    def glu(blk):
        blk = blk.astype(F32)
        return blk[:, :wb_width] * _sigmoid(blk[:, wb_width:])

    _conv_fill(scr, glu(gp_ref[0]), glu(gm_ref[0]), glu(gn_ref[0]), i, last, tl, ntaps)
    cw = cw_ref[...]
    cb = cb_ref[...]
    lg = lg_ref[...]
    lb = lb_ref[...]

    def body(r, carry):
        r0 = pl.multiple_of(r * rb, rb)
        acc = _conv_rows(scr, cw, r0, rb, ntaps) + cb
        mu = jnp.mean(acc, axis=-1, keepdims=True)
        xc = acc - mu
        var = jnp.mean(xc * xc, axis=-1, keepdims=True)
        y = xc * lax.rsqrt(var + EPS) * lg + lb
        act[pl.ds(r0, rb), :] = _silu(y).astype(BF16)
        return carry

    lax.fori_loop(0, tl // rb, body, 0)
    y_b = _dot(act[...], wb_ref[...])

    g_a = _sigmoid(ga_ref[0].astype(F32))
    g_b = _sigmoid(gb_ref[0].astype(F32))
    m_ref[0] = (g_a * y_a + g_b * y_b).astype(BF16)


def _mix(o_f, o_b, p_big, head_norm, conv_w, conv_b, ln_g, ln_b, w_a, w_b, width, wb_width, d):
    b, l, _ = o_f.shape
    tl = min(256, l)
    rb = 32
    hr = BF16_ROWS
    nhb = l // hr
    ntaps = conv_w.shape[0]
    z_blk = 3
    glu_blk = (4 * width) // (2 * wb_width)
    gate_blk = (4 * width + 2 * wb_width) // d
    assert (4 * width) % (2 * wb_width) == 0 and (4 * width + 2 * wb_width) % d == 0
    kern = functools.partial(_mix_kernel, tl=tl, rb=rb, ntaps=ntaps, wb_width=wb_width)
    const = lambda shape: pl.BlockSpec(shape, lambda bi, i: (0, 0))
    return pl.pallas_call(
        kern,
        grid=(b, l // tl),
        in_specs=[
            pl.BlockSpec((1, tl, width), lambda bi, i: (bi, i, 0)),
            pl.BlockSpec((1, tl, width), lambda bi, i: (bi, i, 0)),
            pl.BlockSpec((1, tl, width), lambda bi, i: (bi, i, z_blk)),
            pl.BlockSpec((1, tl, 2 * wb_width), lambda bi, i: (bi, i, glu_blk)),
            pl.BlockSpec((1, hr, 2 * wb_width),
                         lambda bi, i: (bi, jnp.maximum(i * (tl // hr) - 1, 0), glu_blk)),
            pl.BlockSpec((1, hr, 2 * wb_width),
                         lambda bi, i: (bi, jnp.minimum((i + 1) * (tl // hr), nhb - 1), glu_blk)),
            pl.BlockSpec((1, tl, d), lambda bi, i: (bi, i, gate_blk)),
            pl.BlockSpec((1, tl, d), lambda bi, i: (bi, i, gate_blk + 1)),
            const((1, HEAD_DIM)),
            const((ntaps, wb_width)), const((1, wb_width)), const((1, wb_width)), const((1, wb_width)),
            const((width, d)), const((wb_width, d)),
        ],
        out_specs=pl.BlockSpec((1, tl, d), lambda bi, i: (bi, i, 0)),
        out_shape=jax.ShapeDtypeStruct((b, l, d), BF16),
        scratch_shapes=[pltpu.VMEM((len(_conv_shifts(ntaps)), tl + 2 * hr, wb_width), F32),
                        pltpu.VMEM((tl, wb_width), BF16)],
        compiler_params=_cparams(("parallel", "parallel"), 48),
        name="mix",
    )(o_f, o_b, p_big, p_big, p_big, p_big, p_big, p_big,
      head_norm, conv_w, conv_b, ln_g, ln_b, w_a, w_b)


def _outproj_kernel(m_ref, x_ref, wo_ref, nf_ref, wr_ref, x1_ref, h2_ref, aff_ref):
    x1 = x_ref[0] + _dot(m_ref[0], wo_ref[...])
    x1_ref[0] = x1
    ms = jnp.mean(x1 * x1, axis=-1, keepdims=True)
    h2 = x1 * lax.rsqrt(ms + EPS) * nf_ref[...]
    h2_ref[0] = h2
    logits = _dot_hi(h2, wr_ref[...])
    mx = jnp.max(logits, axis=-1, keepdims=True)
    ex = jnp.exp(logits - mx)
    aff_ref[0] = ex / jnp.sum(ex, axis=-1, keepdims=True)


def _outproj(merged, x, w_out, norm_ffn, w_router):
    b, l, d = x.shape
    ne = w_router.shape[1]
    tl = min(256, l)
    const = lambda shape: pl.BlockSpec(shape, lambda bi, i: (0, 0))
    tok = lambda last: pl.BlockSpec((1, tl, last), lambda bi, i: (bi, i, 0))
    return pl.pallas_call(
        _outproj_kernel,
        grid=(b, l // tl),
        in_specs=[tok(d), tok(d), const((d, d)), const((1, d)), const((d, ne))],
        out_specs=[tok(d), tok(d), tok(ne)],
        out_shape=[
            jax.ShapeDtypeStruct((b, l, d), F32),
            jax.ShapeDtypeStruct((b, l, d), F32),
            jax.ShapeDtypeStruct((b, l, ne), F32),
        ],
        compiler_params=_cparams(("parallel", "parallel"), 48),
        name="outproj",
    )(merged, x, w_out, norm_ffn, w_router)


def _route_kernel(a_ref, idx_ref, pos_ref, rs_ref, *, cap, nrows):
    a = a_ref[0]
    bits = lax.bitcast_convert_type(a, I32)
    capf = jnp.float32(cap)

    def count(mask):
        return jnp.sum(jnp.where(mask, 1.0, 0.0), keepdims=True)

    thr = jnp.zeros((1, 1), I32)
    for bit in range(30, -1, -1):
        cand = thr | (1 << bit)
        thr = jnp.where(count(bits >= cand) >= capf, cand, thr)

    upper = _bf01(_iota((LANES, LANES), 0) <= _iota((LANES, LANES), 1))
    ones_ll = _ones_bf16((LANES, LANES))
    strict_lower = _bf01(_iota((nrows, nrows), 1) < _iota((nrows, nrows), 0))

    def prefix(m01):
        within = _dot(m01, upper)
        tot = _dot(m01, ones_ll)
        before = _dot(strict_lower, tot.astype(BF16))
        return within + before

    gt = bits > thr
    eq = bits == thr
    need = capf - count(gt)
    eq_rank = prefix(_bf01(eq))
    mask = gt | (eq & (eq_rank <= need))
    m01 = _bf01(mask)
    incl = prefix(m01)
    pos_ref[0] = jnp.where(mask, incl - 1.0, -1.0).astype(I32)

    ones_8l = _ones_bf16((8, LANES))
    tot_row = _dot_nt(ones_8l, m01)
    strict_upper = _bf01(_iota((nrows, nrows), 0) < _iota((nrows, nrows), 1))
    start_row = _dot(tot_row.astype(BF16), strict_upper)
    end_row = start_row + tot_row
    rs_ref[0] = start_row[0:1, :].astype(I32)

    s_col = _iota((cap, nrows), 0).astype(F32)
    g01 = _bf01((start_row[0:1, :] <= s_col) & (s_col < end_row[0:1, :]))
    hi = jnp.floor(incl * (1.0 / 256.0))
    lo = incl - 256.0 * hi
    rank_at = 256.0 * _dot(g01, hi.astype(BF16)) + _dot(g01, lo.astype(BF16))
    s_lane = _iota((cap, LANES), 0).astype(F32)
    ind = _bf01(rank_at <= s_lane)
    local_row = _dot_nt(ones_8l, ind)
    rvals = _iota((8, nrows), 1).astype(F32).astype(BF16)
    row_of = _dot_nt(rvals, g01)
    idx_ref[0] = (row_of[0:1, :] * float(LANES) + local_row[0:1, :]).astype(I32)


def _route(aff_t3, cap):
    ne, nrows, _ = aff_t3.shape
    assert nrows <= 256 and cap % LANES == 0
    kern = functools.partial(_route_kernel, cap=cap, nrows=nrows)
    return pl.pallas_call(
        kern,
        grid=(ne,),
        in_specs=[pl.BlockSpec((1, nrows, LANES), lambda e: (e, 0, 0))],
        out_specs=[
            pl.BlockSpec((1, 1, cap), lambda e: (e, 0, 0)),
            pl.BlockSpec((1, nrows, LANES), lambda e: (e, 0, 0)),
            pl.BlockSpec((1, 1, nrows), lambda e: (e, 0, 0)),
        ],
        out_shape=[
            jax.ShapeDtypeStruct((ne, 1, cap), I32),
            jax.ShapeDtypeStruct((ne, nrows, LANES), I32),
            jax.ShapeDtypeStruct((ne, 1, nrows), I32),
        ],
        compiler_params=_cparams(("parallel",), 48),
        name="route",
    )(aff_t3)


def _expert_kernel(idx_ref, h2_hbm, wg_ref, wu_ref, wd_ref, ye_ref, buf, sem, *, ts, nreal, nfc):
    step = pl.program_id(0)
    slot = step % 2

    def row_copy(st, sl, r):
        t = idx_ref[st * ts + r]
        return pltpu.make_async_copy(h2_hbm.at[pl.ds(t, 1), :], buf.at[sl, pl.ds(r, 1), :], sem.at[sl])

    def wait_tile(sl):
        pltpu.make_async_copy(h2_hbm.at[pl.ds(0, ts), :], buf.at[sl], sem.at[sl]).wait()

    @pl.when(step == 0)
    def _():
        def body(r, carry):
            row_copy(0, 0, r).start()
            return carry
        lax.fori_loop(0, ts, body, 0)

    @pl.when(step < nreal)
    def _():
        wait_tile(slot)
        x = buf[slot].astype(BF16)
        nxt = jnp.minimum(step + 1, nreal - 1)
        rows_per = ts // nfc
        fc = wg_ref.shape[2] // nfc
        y = None
        for c in range(nfc):
            for r in range(c * rows_per, (c + 1) * rows_per):
                row_copy(nxt, 1 - slot, r).start()
            a = _dot(x, wg_ref[0, :, c * fc:(c + 1) * fc])
            b = _dot(x, wu_ref[0, :, c * fc:(c + 1) * fc])
            hm = (_silu(a) * b).astype(BF16)
            t = _dot(hm, wd_ref[0, c * fc:(c + 1) * fc, :])
            y = t if y is None else y + t
        ye_ref[...] = y.astype(BF16)

        @pl.when(step == nreal - 1)
        def _():
            wait_tile(1 - slot)

    @pl.when(step >= nreal)
    def _():
        ye_ref[...] = jnp.zeros_like(ye_ref)


def _experts(idx_flat, h2, w_gate, w_up, w_down, cap):
    n, d = h2.shape
    ne, _, f = w_gate.shape
    ts = min(256, cap)
    nj = cap // ts
    nreal = ne * nj
    nfc = 4
    assert ts % nfc == 0 and f % nfc == 0
    kern = functools.partial(_expert_kernel, ts=ts, nreal=nreal, nfc=nfc)
    w_idx = lambda s, idx: (jnp.minimum(s // nj, ne - 1), 0, 0)
    grid_spec = pltpu.PrefetchScalarGridSpec(
        num_scalar_prefetch=1,
        grid=(nreal + 1,),
        in_specs=[
            pl.BlockSpec(memory_space=pl.ANY),
            pl.BlockSpec((1, d, f), w_idx),
            pl.BlockSpec((1, d, f), w_idx),
            pl.BlockSpec((1, f, d), w_idx),
        ],
        out_specs=pl.BlockSpec((ts, d), lambda s, idx: (s, 0)),
        scratch_shapes=[pltpu.VMEM((2, ts, d), F32), pltpu.SemaphoreType.DMA((2,))],
    )
    return pl.pallas_call(
        kern,
        grid_spec=grid_spec,
        out_shape=jax.ShapeDtypeStruct((ne * cap + ts, d), BF16),
        compiler_params=_cparams(("arbitrary",), 56),
        name="experts",
    )(idx_flat, h2, w_gate, w_up, w_down)


def _combine_kernel(ts_ref, x1_ref, pos_ref, aff_ref, nf_ref, ye_hbm, y_ref, buf, sem,
                    *, tt, cap, ne, piece):
    i = pl.program_id(0)
    nsteps = pl.num_programs(0)
    slot = i % 2
    br = tt + BF16_ROWS
    pieces = [(0, piece), (piece, piece), (2 * piece, br - 2 * piece)]
    assert br > 2 * piece

    def start_of(st, e):
        first = e * cap + ts_ref[st * ne + e]
        return pl.multiple_of(jnp.bitwise_and(first, -BF16_ROWS), BF16_ROWS)

    def rows_needed(st, e):
        first = e * cap + ts_ref[st * ne + e]
        return first - start_of(st, e) + ts_ref[(st + 1) * ne + e] - ts_ref[st * ne + e]

    def copy(st, sl, e, p):
        off, size = pieces[p]
        return pltpu.make_async_copy(ye_hbm.at[pl.ds(start_of(st, e) + off, size), :],
                                     buf.at[sl, pl.ds(e * br + off, size), :], sem.at[sl])

    def for_each_copy(st, sl, fn):
        for e in range(ne):
            fn(copy(st, sl, e, 0))
        for e in range(ne):
            for p in range(1, len(pieces)):
                @pl.when(rows_needed(st, e) > pieces[p][0])
                def _():
                    fn(copy(st, sl, e, p))

    @pl.when(i == 0)
    def _():
        buf[...] = jnp.zeros_like(buf)
        for_each_copy(0, 0, lambda c: c.start())

    @pl.when(i + 1 < nsteps)
    def _():
        for_each_copy(i + 1, 1 - slot, lambda c: c.start())

    for_each_copy(i, slot, lambda c: c.wait())

    pos = pos_ref[...]
    aff = aff_ref[...]
    lane_e = _iota((1, ne), 1)
    base = jnp.zeros((1, ne), I32)
    for e in range(ne):
        base = jnp.where(lane_e == e, start_of(i, e) - e * cap, base)
    rel = jnp.clip(pos - base, -1, br).astype(F32)
    kdim = ne * br
    c_idx = _iota((ne, kdim), 1)
    e_idx = _iota((ne, kdim), 0)
    expand = _bf01((c_idx >= e_idx * br) & (c_idx < (e_idx + 1) * br))
    e_of_c = jnp.sum(jnp.where(c_idx >= (e_idx + 1) * br, 1.0, 0.0), axis=0, keepdims=True)
    k_of_c = _iota((1, kdim), 1).astype(F32) - float(br) * e_of_c
    rel_x = _dot(rel.astype(BF16), expand)
    aff_x = _dot(aff.astype(BF16), expand)
    emat = jnp.where(rel_x == k_of_c, aff_x, 0.0).astype(BF16)
    out = x1_ref[...] + _dot(emat, buf[slot])
    ms = jnp.mean(out * out, axis=-1, keepdims=True)
    y_ref[...] = out * lax.rsqrt(ms + EPS) * nf_ref[...]


def _combine(tile_starts, x1, pos_tm, aff, norm_final, ye, cap):
    n, d = x1.shape
    ne = aff.shape[1]
    tt = LANES
    br = tt + BF16_ROWS
    assert cap % BF16_ROWS == 0 and ye.shape[0] >= ne * cap + br
    kern = functools.partial(_combine_kernel, tt=tt, cap=cap, ne=ne, piece=64)
    grid_spec = pltpu.PrefetchScalarGridSpec(
        num_scalar_prefetch=1,
        grid=(n // tt,),
        in_specs=[
            pl.BlockSpec((tt, d), lambda i, ts: (i, 0)),
            pl.BlockSpec((tt, ne), lambda i, ts: (i, 0)),
            pl.BlockSpec((tt, ne), lambda i, ts: (i, 0)),
            pl.BlockSpec((1, d), lambda i, ts: (0, 0)),
            pl.BlockSpec(memory_space=pl.ANY),
        ],
        out_specs=pl.BlockSpec((tt, d), lambda i, ts: (i, 0)),
        scratch_shapes=[pltpu.VMEM((2, ne * br, d), BF16), pltpu.SemaphoreType.DMA((2,))],
    )
    return pl.pallas_call(
        kern,
        grid_spec=grid_spec,
        out_shape=jax.ShapeDtypeStruct((n, d), F32),
        compiler_params=_cparams(("arbitrary",), 48),
        name="combine",
    )(tile_starts, x1, pos_tm, aff, norm_final, ye)


def _trunk(x, w, nh, width, wb_width):
    b, l, d = x.shape
    n = b * l
    ne = w["w_router"].shape[1]
    cap = max(1, CAPACITY_FACTOR * n // ne)

    p_big, ab = _inproj(x, w["norm_mix"], w["w_big"], w["w_small"], width)
    qkv = _prep(p_big, w["conv_a_w"], width)
    gcum, beta, gcum_t = _gates(ab, w["a_log"], w["dt_bias"], nh)
    o_f, o_b = _gdn2(qkv, gcum, beta, gcum_t, nh)
    merged = _mix(o_f, o_b, p_big, w["head_norm"], w["conv_b_w"], w["conv_b_b"], w["ln_b_g"], w["ln_b_b"],
                  w["w_proj_a"], w["w_proj_b"], width, wb_width, d)
    x1, h2, aff = _outproj(merged, x, w["w_out"], w["norm_ffn"], w["w_router"])

    aff = aff.reshape(n, ne)
    nrows = n // LANES
    idx, pos, row_start = _route(aff.T.reshape(ne, nrows, LANES), cap)
    idx_flat = idx.reshape(ne * cap)
    pos_tm = pos.reshape(ne, n).T
    tile_starts = jnp.concatenate(
        [row_start.reshape(ne, nrows).T, jnp.full((1, ne), cap, I32)], axis=0).reshape(-1)

    ye = _experts(idx_flat, h2.reshape(n, d), w["w_gate"], w["w_up"], w["w_down"], cap)
    y = _combine(tile_starts, x1.reshape(n, d), pos_tm, aff, w["norm_final"], ye, cap)
    return y.reshape(b, l, d)


def kernel(x_prompt, x_sample, norm_mix, w_in, conv_a_w, a_log, dt_bias, head_norm, w_proj_a,
           conv_b_w, conv_b_b, ln_b_g, ln_b_b, w_proj_b, w_out, norm_ffn, w_router, w_gate, w_up,
           w_down, norm_final):
    assert w_in.shape[0] == 1, "single layer"
    width = conv_a_w.shape[-1] // 3
    wb_width = conv_b_w.shape[-1]
    nh = width // HEAD_DIM
    off_small = 4 * width
    off_glu = off_small + 4 * nh
    w_in0 = w_in[0]
    row = lambda v: v.reshape(1, -1)
    w = {
        "norm_mix": row(norm_mix[0]),
        "w_big": jnp.concatenate([w_in0[:, :off_small], w_in0[:, off_glu:]], axis=1).astype(BF16),
        "w_small": w_in0[:, off_small:off_glu],
        "conv_a_w": conv_a_w[0],
        "a_log": a_log[0],
        "dt_bias": dt_bias[0],
        "head_norm": row(head_norm[0]),
        "w_proj_a": w_proj_a[0].astype(BF16),
        "conv_b_w": conv_b_w[0],
        "conv_b_b": row(conv_b_b[0]),
        "ln_b_g": row(ln_b_g[0]),
        "ln_b_b": row(ln_b_b[0]),
        "w_proj_b": w_proj_b[0].astype(BF16),
        "w_out": w_out[0].astype(BF16),
        "norm_ffn": row(norm_ffn[0]),
        "w_router": w_router[0],
        "w_gate": w_gate[0].astype(BF16),
        "w_up": w_up[0].astype(BF16),
        "w_down": w_down[0].astype(BF16),
        "norm_final": row(norm_final),
    }
    y_prompt = _trunk(x_prompt, w, nh, width, wb_width)
    y_sample = _trunk(x_sample, w, nh, width, wb_width)
    return (y_prompt, y_sample)
```

```python
import functools

import jax
import jax.numpy as jnp
from jax import lax
from jax.experimental import pallas as pl
from jax.experimental.pallas import tpu as pltpu

F32 = jnp.float32
BF16 = jnp.bfloat16
I32 = jnp.int32

HEAD_DIM = 128
CHUNK = 64
CHUNK_SHIFT = 6
CAPACITY_FACTOR = 2
EPS = 1e-6
LANES = 128
BF16_ROWS = 16
MIB = 1024 * 1024


def _cparams(sem, vmem_mib):
    return pltpu.CompilerParams(dimension_semantics=sem, vmem_limit_bytes=vmem_mib * MIB)


def _dot(a, b):
    return jnp.dot(a, b, preferred_element_type=F32)


def _dot_nt(a, b):
    return lax.dot_general(a, b, (((1,), (1,)), ((), ())), preferred_element_type=F32)


def _dot_tn(a, b):
    return lax.dot_general(a, b, (((0,), (0,)), ((), ())), preferred_element_type=F32)


def _split2(a):
    hi = a.astype(BF16)
    lo = (a - hi.astype(F32)).astype(BF16)
    return hi, lo


def _split3(a):
    hi = a.astype(BF16)
    r = a - hi.astype(F32)
    mid = r.astype(BF16)
    lo = (r - mid.astype(F32)).astype(BF16)
    return hi, mid, lo


def _dot_hi(a, b):
    ah, al = _split2(a)
    bh, bl = _split2(b)
    n = b.shape[1]
    both = _dot(ah, jnp.concatenate([bh, bl], axis=1))
    return both[:, :n] + both[:, n:] + _dot(al, bh)


def _sigmoid(x):
    return 0.5 * jnp.tanh(0.5 * x) + 0.5


def _silu(x):
    return x * _sigmoid(x)


def _softplus(x):
    return jnp.maximum(x, 0.0) + jnp.log(1.0 + jnp.exp(-jnp.abs(x)))


def _iota(shape, dim):
    return lax.broadcasted_iota(I32, shape, dim)


def _ones_bf16(shape):
    return jnp.ones(shape, BF16)


def _bf01(mask):
    return jnp.where(mask, 1.0, 0.0).astype(BF16)


def _inproj_kernel(x_ref, g_ref, wb_ref, ws_ref, p_ref, ab_ref, h_scr):
    @pl.when(pl.program_id(2) == 0)
    def _():
        rows = min(256, h_scr.shape[0])
        for r in range(h_scr.shape[0] // rows):
            x = x_ref[0, r * rows:(r + 1) * rows, :]
            ms = jnp.mean(x * x, axis=-1, keepdims=True)
            h = x * lax.rsqrt(ms + EPS) * g_ref[...]
            h_scr[r * rows:(r + 1) * rows, :] = h.astype(BF16)
            ab_ref[0, r * rows:(r + 1) * rows, :] = _dot_hi(h, ws_ref[...])

    p_ref[0] = _dot(h_scr[...], wb_ref[...]).astype(BF16)


def _inproj(x, g, w_big, w_small, tn):
    b, l, d = x.shape
    nbig = w_big.shape[1]
    nsmall = w_small.shape[1]
    tm = min(1024, l)
    return pl.pallas_call(
        _inproj_kernel,
        grid=(b, l // tm, nbig // tn),
        in_specs=[
            pl.BlockSpec((1, tm, d), lambda bi, i, j: (bi, i, 0)),
            pl.BlockSpec((1, d), lambda bi, i, j: (0, 0)),
            pl.BlockSpec((d, tn), lambda bi, i, j: (0, j)),
            pl.BlockSpec((d, nsmall), lambda bi, i, j: (0, 0)),
        ],
        out_specs=[
            pl.BlockSpec((1, tm, tn), lambda bi, i, j: (bi, i, j)),
            pl.BlockSpec((1, tm, nsmall), lambda bi, i, j: (bi, i, 0)),
        ],
        out_shape=[
            jax.ShapeDtypeStruct((b, l, nbig), BF16),
            jax.ShapeDtypeStruct((b, l, nsmall), F32),
        ],
        scratch_shapes=[pltpu.VMEM((tm, d), BF16)],
        compiler_params=_cparams(("parallel", "parallel", "arbitrary"), 48),
        name="inproj",
    )(x, g, w_big, w_small)


SUBLANES = 8


def _conv_shifts(ntaps):
    first = BF16_ROWS - ntaps // 2
    return sorted({(first + t) % SUBLANES for t in range(ntaps)})


def _conv_fill(scr, prev, main, nxt, i, last, tl, ntaps):
    hr = BF16_ROWS
    shifts = _conv_shifts(ntaps)
    rows = tl + 2 * hr
    base = shifts.index(0)
    scr[base, 0:hr, :] = jnp.where(i > 0, prev, 0.0)
    scr[base, hr:hr + tl, :] = main
    scr[base, hr + tl:rows, :] = jnp.where(i < last, nxt, 0.0)
    full = scr[base]
    for c, s in enumerate(shifts):
        if s:
            scr[c] = pltpu.roll(full, rows - s, axis=0)


def _conv_rows(scr, w, r0, rb, ntaps):
    shifts = _conv_shifts(ntaps)
    first = BF16_ROWS - ntaps // 2
    acc = None
    for t in range(ntaps):
        off = first + t
        s = off % SUBLANES
        win = scr[shifts.index(s), pl.ds(r0 + (off - s), rb), :]
        term = win * w[t:t + 1, :]
        acc = term if acc is None else acc + term
    return acc


def _prep_kernel(pm_ref, pp_ref, pn_ref, cw_ref, o_ref, scr, *, tl, rb, ntaps):
    i = pl.program_id(1)
    j = pl.program_id(2)
    last = pl.num_programs(1) - 1
    _conv_fill(scr, pp_ref[0].astype(F32), pm_ref[0].astype(F32), pn_ref[0].astype(F32),
               i, last, tl, ntaps)
    w = cw_ref[...]
    width = w.shape[1]
    nh = width // HEAD_DIM
    qscale = jnp.where(j == 0, HEAD_DIM ** -0.5, 1.0).astype(F32)
    is_qk = j < 2

    def body(r, carry):
        r0 = pl.multiple_of(r * rb, rb)
        y = _silu(_conv_rows(scr, w, r0, rb, ntaps))
        outs = []
        for h in range(nh):
            yh = y[:, h * HEAD_DIM:(h + 1) * HEAD_DIM]
            ss = jnp.sum(yh * yh, axis=-1, keepdims=True)
            fac = jnp.where(is_qk, lax.rsqrt(ss + EPS) * qscale, 1.0)
            outs.append(yh * fac)
        o_ref[0, 0, pl.ds(r0, rb), :] = jnp.concatenate(outs, axis=-1).astype(BF16)
        return carry

    lax.fori_loop(0, tl // rb, body, 0)


def _prep(p_big, conv_w, width):
    b, l, _ = p_big.shape
    tl = min(512, l)
    rb = 32
    hr = BF16_ROWS
    nhb = l // hr
    ntaps = conv_w.shape[0]
    assert ntaps // 2 < hr
    kern = functools.partial(_prep_kernel, tl=tl, rb=rb, ntaps=ntaps)
    return pl.pallas_call(
        kern,
        grid=(b, l // tl, 3),
        in_specs=[
            pl.BlockSpec((1, tl, width), lambda bi, i, j: (bi, i, j)),
            pl.BlockSpec((1, hr, width), lambda bi, i, j: (bi, jnp.maximum(i * (tl // hr) - 1, 0), j)),
            pl.BlockSpec((1, hr, width), lambda bi, i, j: (bi, jnp.minimum((i + 1) * (tl // hr), nhb - 1), j)),
            pl.BlockSpec((ntaps, width), lambda bi, i, j: (0, j)),
        ],
        out_specs=pl.BlockSpec((1, 1, tl, width), lambda bi, i, j: (j, bi, i, 0)),
        out_shape=jax.ShapeDtypeStruct((3, b, l, width), BF16),
        scratch_shapes=[pltpu.VMEM((len(_conv_shifts(ntaps)), tl + 2 * hr, width), F32)],
        compiler_params=_cparams(("parallel", "parallel", "arbitrary"), 32),
        name="prep",
    )(p_big, p_big, p_big, conv_w)


def _gates_kernel(ab_ref, at_ref, arow_ref, dtrow_ref, acol_ref, dtcol_ref,
                  gc_ref, bt_ref, gct_ref, *, tl, nh):
    ab = ab_ref[0]
    beta = _sigmoid(ab[:, :2 * nh])
    alpha = ab[:, 2 * nh:]
    g = -jnp.exp(arow_ref[...]) * _softplus(alpha + dtrow_ref[...])
    r = _iota((tl, tl), 0)
    c = _iota((tl, tl), 1)
    same = jnp.right_shift(r, CHUNK_SHIFT) == jnp.right_shift(c, CHUNK_SHIFT)
    m_f = _bf01(same & (c <= r))
    m_b = _bf01(same & (c >= r))
    g3 = _split3(g)
    gf = _dot(m_f, g3[0]) + _dot(m_f, g3[1]) + _dot(m_f, g3[2])
    gb = _dot(m_b, g3[0]) + _dot(m_b, g3[1]) + _dot(m_b, g3[2])
    gc_ref[0, 0] = gf[:, :nh]
    gc_ref[0, 1] = gb[:, nh:]
    bt_ref[0, 0] = beta[:, :nh]
    bt_ref[0, 1] = beta[:, nh:]
    nc = tl // CHUNK
    at = at_ref[0]
    gt = -jnp.exp(acol_ref[...])[None] * _softplus(at + dtcol_ref[...][None])
    gt2 = gt.reshape(nc * 2 * nh, CHUNK)
    rr = _iota((CHUNK, CHUNK), 0)
    cc = _iota((CHUNK, CHUNK), 1)
    u_f = _bf01(rr <= cc)
    u_b = _bf01(rr >= cc)
    t3 = _split3(gt2)
    cf = (_dot(t3[0], u_f) + _dot(t3[1], u_f) + _dot(t3[2], u_f)).reshape(nc, 2 * nh, CHUNK)
    cb = (_dot(t3[0], u_b) + _dot(t3[1], u_b) + _dot(t3[2], u_b)).reshape(nc, 2 * nh, CHUNK)
    gct_ref[0, 0] = cf[:, :nh, :]
    gct_ref[0, 1] = cb[:, nh:, :]


def _gates(ab, a_log, dt_bias, nh):
    b, l, _ = ab.shape
    tl = min(512, l)
    nc = tl // CHUNK
    at = ab[..., 2 * nh:].reshape(b, l // CHUNK, CHUNK, 2 * nh).swapaxes(-1, -2)
    arow = a_log.reshape(1, 2 * nh)
    dtrow = dt_bias.reshape(1, 2 * nh)
    acol = a_log.reshape(2 * nh, 1)
    dtcol = dt_bias.reshape(2 * nh, 1)
    kern = functools.partial(_gates_kernel, tl=tl, nh=nh)
    small = lambda shape: pl.BlockSpec(shape, lambda bi, i: (0, 0))
    return pl.pallas_call(
        kern,
        grid=(b, l // tl),
        in_specs=[
            pl.BlockSpec((1, tl, 4 * nh), lambda bi, i: (bi, i, 0)),
            pl.BlockSpec((1, nc, 2 * nh, CHUNK), lambda bi, i: (bi, i, 0, 0)),
            small((1, 2 * nh)), small((1, 2 * nh)), small((2 * nh, 1)), small((2 * nh, 1)),
        ],
        out_specs=[
            pl.BlockSpec((1, 2, tl, nh), lambda bi, i: (bi, 0, i, 0)),
            pl.BlockSpec((1, 2, tl, nh), lambda bi, i: (bi, 0, i, 0)),
            pl.BlockSpec((1, 2, nc, nh, CHUNK), lambda bi, i: (bi, 0, i, 0, 0)),
        ],
        out_shape=[
            jax.ShapeDtypeStruct((b, 2, l, nh), F32),
            jax.ShapeDtypeStruct((b, 2, l, nh), F32),
            jax.ShapeDtypeStruct((b, 2, l // CHUNK, nh, CHUNK), F32),
        ],
        compiler_params=_cparams(("parallel", "parallel"), 32),
        name="gates",
    )(ab, at, arow, dtrow, acol, dtcol)


def _gdn2_kernel(qf_ref, kf_ref, vf_ref, qb_ref, kb_ref, vb_ref,
                 gcf_ref, btf_ref, gctf_ref, gcb_ref, btb_ref, gctb_ref,
                 of_ref, ob_ref, s_scr, u_scr, wq_scr, ke_scr, qk_scr, *, tl, nh):
    i = pl.program_id(1)

    @pl.when(i == 0)
    def _():
        s_scr[...] = jnp.zeros_like(s_scr)

    nc = tl // CHUNK
    row = _iota((CHUNK, CHUNK), 0)
    col = _iota((CHUNK, CHUNK), 1)
    m_incl = [row >= col, row <= col]
    m_strict = [row > col, row < col]
    eye = jnp.where(row == col, 1.0, 0.0).astype(F32)
    refs = [dict(q=qf_ref, k=kf_ref, v=vf_ref, gc=gcf_ref, bt=btf_ref, gct=gctf_ref, o=of_ref),
            dict(q=qb_ref, k=kb_ref, v=vb_ref, gc=gcb_ref, bt=btb_ref, gct=gctb_ref, o=ob_ref)]
    streams = range(2 * nh)
    dr = [s // nh for s in streams]
    hd = [s % nh for s in streams]
    sl = [slice(hd[s] * HEAD_DIM, (hd[s] + 1) * HEAD_DIM) for s in streams]

    def chunk_rows(d, ci):
        c = ci if d == 0 else nc - 1 - ci
        return c, pl.multiple_of(c * CHUNK, CHUNK)

    def last_row(d, gcols):
        return gcols[CHUNK - 1:CHUNK, :] if d == 0 else gcols[0:1, :]

    def local_begin(ci):
        cr = [chunk_rows(d, ci) for d in range(2)]
        gcols = [refs[d]["gc"][0, 0, pl.ds(cr[d][1], CHUNK), :] for d in range(2)]
        bcols = [refs[d]["bt"][0, 0, pl.ds(cr[d][1], CHUNK), :] for d in range(2)]
        grows = [refs[d]["gct"][0, 0, cr[d][0]] for d in range(2)]
        glast = [last_row(d, gcols[d]) for d in range(2)]
        q = [refs[dr[s]]["q"][0, 0, pl.ds(cr[dr[s]][1], CHUNK), sl[s]] for s in streams]
        k16 = [refs[dr[s]]["k"][0, 0, pl.ds(cr[dr[s]][1], CHUNK), sl[s]] for s in streams]
        k = [k16[s].astype(F32) for s in streams]
        v = [refs[dr[s]]["v"][0, 0, pl.ds(cr[dr[s]][1], CHUNK), sl[s]].astype(F32) for s in streams]
        gcol = [gcols[dr[s]][:, hd[s]:hd[s] + 1] for s in streams]
        beta = [bcols[dr[s]][:, hd[s]:hd[s] + 1] for s in streams]
        grow = [grows[dr[s]][hd[s]:hd[s] + 1, :] for s in streams]
        gl = [glast[dr[s]][:, hd[s]:hd[s] + 1] for s in streams]
        kb = [k[s] * beta[s] for s in streams]
        kk = [_dot_nt(kb[s].astype(BF16), k16[s]) for s in streams]
        qk = [_dot_nt(q[s], k16[s]) for s in streams]
        return dict(grow=grow, q=q, k=k, v=v, gcol=gcol, beta=beta, gl=gl, kb=kb, kk=kk, qk=qk)

    def local_finish(st, slot):
        gcol, beta, kb = st["gcol"], st["beta"], st["kb"]
        dec = [jnp.where(m_incl[dr[s]],
                         jnp.exp(jnp.where(m_incl[dr[s]], gcol[s] - st["grow"][s], 0.0)), 0.0) for s in streams]
        pw = [jnp.where(m_strict[dr[s]], -(st["kk"][s] * dec[s]), 0.0) for s in streams]
        p16 = [pw[s].astype(BF16) for s in streams]
        tm = [eye + pw[s] for s in streams]
        for _ in range(5):
            pw = [_dot(p16[s], p16[s]) for s in streams]
            p16 = [pw[s].astype(BF16) for s in streams]
            tm = [tm[s] + _dot(tm[s].astype(BF16), p16[s]) for s in streams]
        eg = [jnp.exp(gcol[s]) for s in streams]
        rhs = [jnp.concatenate([st["v"][s] * beta[s], kb[s] * eg[s]], axis=-1).astype(BF16) for s in streams]
        sol = [_dot(tm[s].astype(BF16), rhs[s]) for s in streams]
        for s in streams:
            u_scr[slot, s] = sol[s][:, :HEAD_DIM]
            wq_scr[slot, s, 0:CHUNK, :] = sol[s][:, HEAD_DIM:].astype(BF16)
            wq_scr[slot, s, CHUNK:2 * CHUNK, :] = (st["q"][s].astype(F32) * eg[s]).astype(BF16)
            ke_scr[slot, s] = (st["k"][s] * jnp.exp(st["gl"][s] - gcol[s])).astype(BF16)
            qk_scr[slot, s] = (st["qk"][s] * dec[s]).astype(BF16)

    def scan_begin(slot):
        state = [s_scr[s] for s in streams]
        ws = [_dot(wq_scr[slot, s], state[s].astype(BF16)) for s in streams]
        return state, ws

    def scan_finish(ci, slot, state, ws):
        cr = [chunk_rows(d, ci) for d in range(2)]
        glast = [last_row(d, refs[d]["gc"][0, 0, pl.ds(cr[d][1], CHUNK), :]) for d in range(2)]
        vn16 = [(u_scr[slot, s] - ws[s][:CHUNK]).astype(BF16) for s in streams]
        s_add = [_dot_tn(ke_scr[slot, s], vn16[s]) for s in streams]
        o_in = [_dot(qk_scr[slot, s], vn16[s]) for s in streams]
        for s in streams:
            s_scr[s] = state[s] * jnp.exp(glast[dr[s]][:, hd[s]:hd[s] + 1]) + s_add[s]
            refs[dr[s]]["o"][0, pl.ds(cr[dr[s]][1], CHUNK), sl[s]] = (ws[s][CHUNK:] + o_in[s]).astype(BF16)

    def step(ci, slot):
        state, ws = scan_begin(slot)
        st = local_begin(jnp.minimum(ci + 1, nc - 1))
        scan_finish(ci, slot, state, ws)
        local_finish(st, 1 - slot)

    local_finish(local_begin(0), 0)

    def pair(j, carry):
        step(2 * j, 0)
        step(2 * j + 1, 1)
        return carry

    lax.fori_loop(0, nc // 2, pair, 0)


def _gdn2(qkv, gcum, beta, gcum_t, nh):
    _, b, l, width = qkv.shape
    tl = min(512, l)
    nt = l // tl
    nc = tl // CHUNK
    assert nc % 2 == 0
    kern = functools.partial(_gdn2_kernel, tl=tl, nh=nh)
    fw = lambda i: i
    bw = lambda i: nt - 1 - i
    qkv_spec = lambda which, t: pl.BlockSpec((1, 1, tl, width), lambda bi, i: (which, bi, t(i), 0))
    col_spec = lambda d, t: pl.BlockSpec((1, 1, tl, nh), lambda bi, i: (bi, d, t(i), 0))
    row_spec = lambda d, t: pl.BlockSpec((1, 1, nc, nh, CHUNK), lambda bi, i: (bi, d, t(i), 0, 0))
    ns = 2 * nh
    return pl.pallas_call(
        kern,
        grid=(b, nt),
        in_specs=[
            qkv_spec(0, fw), qkv_spec(1, fw), qkv_spec(2, fw),
            qkv_spec(0, bw), qkv_spec(1, bw), qkv_spec(2, bw),
            col_spec(0, fw), col_spec(0, fw), row_spec(0, fw),
            col_spec(1, bw), col_spec(1, bw), row_spec(1, bw),
        ],
        out_specs=[pl.BlockSpec((1, tl, width), lambda bi, i: (bi, i, 0)),
                   pl.BlockSpec((1, tl, width), lambda bi, i: (bi, nt - 1 - i, 0))],
        out_shape=[jax.ShapeDtypeStruct((b, l, width), BF16)] * 2,
        scratch_shapes=[
            pltpu.VMEM((ns, HEAD_DIM, HEAD_DIM), F32),
            pltpu.VMEM((2, ns, CHUNK, HEAD_DIM), F32),
            pltpu.VMEM((2, ns, 2 * CHUNK, HEAD_DIM), BF16),
            pltpu.VMEM((2, ns, CHUNK, HEAD_DIM), BF16),
            pltpu.VMEM((2, ns, CHUNK, CHUNK), BF16),
        ],
        compiler_params=_cparams(("parallel", "arbitrary"), 48),
        name="gdn",
    )(qkv, qkv, qkv, qkv, qkv, qkv, gcum, beta, gcum_t, gcum, beta, gcum_t)


def _mix_kernel(of_ref, ob_ref, z_ref, gm_ref, gp_ref, gn_ref, ga_ref, gb_ref,
                hn_ref, cw_ref, cb_ref, lg_ref, lb_ref, wa_ref, wb_ref,
                m_ref, scr, act, *, tl, rb, ntaps, wb_width):
    i = pl.program_id(1)
    last = pl.num_programs(1) - 1

    o = of_ref[0].astype(F32) + ob_ref[0].astype(F32)
    width = o.shape[1]
    nh = width // HEAD_DIM
    hn = hn_ref[...]
    parts = []
    for h in range(nh):
        oh = o[:, h * HEAD_DIM:(h + 1) * HEAD_DIM]
        ms = jnp.mean(oh * oh, axis=-1, keepdims=True)
        parts.append(oh * lax.rsqrt(ms + EPS) * hn)
    on = jnp.concatenate(parts, axis=-1)
    ya_in = (on * _silu(z_ref[0].astype(F32))).astype(BF16)
    y_a = _dot(ya_in, wa_ref[...])

    def glu(blk):
        blk = blk.astype(F32)
        return blk[:, :wb_width] * _sigmoid(blk[:, wb_width:])

    _conv_fill(scr, glu(gp_ref[0]), glu(gm_ref[0]), glu(gn_ref[0]), i, last, tl, ntaps)
    cw = cw_ref[...]
    cb = cb_ref[...]
    lg = lg_ref[...]
    lb = lb_ref[...]

    def body(r, carry):
        r0 = pl.multiple_of(r * rb, rb)
        acc = _conv_rows(scr, cw, r0, rb, ntaps) + cb
        mu = jnp.mean(acc, axis=-1, keepdims=True)
        xc = acc - mu
        var = jnp.mean(xc * xc, axis=-1, keepdims=True)
        y = xc * lax.rsqrt(var + EPS) * lg + lb
        act[pl.ds(r0, rb), :] = _silu(y).astype(BF16)
        return carry

    lax.fori_loop(0, tl // rb, body, 0)
    y_b = _dot(act[...], wb_ref[...])

    g_a = _sigmoid(ga_ref[0].astype(F32))
    g_b = _sigmoid(gb_ref[0].astype(F32))
    m_ref[0] = (g_a * y_a + g_b * y_b).astype(BF16)


def _mix(o_f, o_b, p_big, head_norm, conv_w, conv_b, ln_g, ln_b, w_a, w_b, width, wb_width, d):
    b, l, _ = o_f.shape
    tl = min(256, l)
    rb = 32
    hr = BF16_ROWS
    nhb = l // hr
    ntaps = conv_w.shape[0]
    z_blk = 3
    glu_blk = (4 * width) // (2 * wb_width)
    gate_blk = (4 * width + 2 * wb_width) // d
    assert (4 * width) % (2 * wb_width) == 0 and (4 * width + 2 * wb_width) % d == 0
    kern = functools.partial(_mix_kernel, tl=tl, rb=rb, ntaps=ntaps, wb_width=wb_width)
    const = lambda shape: pl.BlockSpec(shape, lambda bi, i: (0, 0))
    return pl.pallas_call(
        kern,
        grid=(b, l // tl),
        in_specs=[
            pl.BlockSpec((1, tl, width), lambda bi, i: (bi, i, 0)),
            pl.BlockSpec((1, tl, width), lambda bi, i: (bi, i, 0)),
            pl.BlockSpec((1, tl, width), lambda bi, i: (bi, i, z_blk)),
            pl.BlockSpec((1, tl, 2 * wb_width), lambda bi, i: (bi, i, glu_blk)),
            pl.BlockSpec((1, hr, 2 * wb_width),
                         lambda bi, i: (bi, jnp.maximum(i * (tl // hr) - 1, 0), glu_blk)),
            pl.BlockSpec((1, hr, 2 * wb_width),
                         lambda bi, i: (bi, jnp.minimum((i + 1) * (tl // hr), nhb - 1), glu_blk)),
            pl.BlockSpec((1, tl, d), lambda bi, i: (bi, i, gate_blk)),
            pl.BlockSpec((1, tl, d), lambda bi, i: (bi, i, gate_blk + 1)),
            const((1, HEAD_DIM)),
            const((ntaps, wb_width)), const((1, wb_width)), const((1, wb_width)), const((1, wb_width)),
            const((width, d)), const((wb_width, d)),
        ],
        out_specs=pl.BlockSpec((1, tl, d), lambda bi, i: (bi, i, 0)),
        out_shape=jax.ShapeDtypeStruct((b, l, d), BF16),
        scratch_shapes=[pltpu.VMEM((len(_conv_shifts(ntaps)), tl + 2 * hr, wb_width), F32),
                        pltpu.VMEM((tl, wb_width), BF16)],
        compiler_params=_cparams(("parallel", "parallel"), 48),
        name="mix",
    )(o_f, o_b, p_big, p_big, p_big, p_big, p_big, p_big,
      head_norm, conv_w, conv_b, ln_g, ln_b, w_a, w_b)


def _outproj_kernel(m_ref, x_ref, wo_ref, nf_ref, wr_ref, x1_ref, h2_ref, aff_ref):
    x1 = x_ref[0] + _dot(m_ref[0], wo_ref[...])
    x1_ref[0] = x1
    ms = jnp.mean(x1 * x1, axis=-1, keepdims=True)
    h2 = x1 * lax.rsqrt(ms + EPS) * nf_ref[...]
    h2_ref[0] = h2
    logits = _dot_hi(h2, wr_ref[...])
    mx = jnp.max(logits, axis=-1, keepdims=True)
    ex = jnp.exp(logits - mx)
    aff_ref[0] = ex / jnp.sum(ex, axis=-1, keepdims=True)


def _outproj(merged, x, w_out, norm_ffn, w_router):
    b, l, d = x.shape
    ne = w_router.shape[1]
    tl = min(256, l)
    const = lambda shape: pl.BlockSpec(shape, lambda bi, i: (0, 0))
    tok = lambda last: pl.BlockSpec((1, tl, last), lambda bi, i: (bi, i, 0))
    return pl.pallas_call(
        _outproj_kernel,
        grid=(b, l // tl),
        in_specs=[tok(d), tok(d), const((d, d)), const((1, d)), const((d, ne))],
        out_specs=[tok(d), tok(d), tok(ne)],
        out_shape=[
            jax.ShapeDtypeStruct((b, l, d), F32),
            jax.ShapeDtypeStruct((b, l, d), F32),
            jax.ShapeDtypeStruct((b, l, ne), F32),
        ],
        compiler_params=_cparams(("parallel", "parallel"), 48),
        name="outproj",
    )(merged, x, w_out, norm_ffn, w_router)


def _route_kernel(a_ref, idx_ref, pos_ref, rs_ref, *, cap, nrows):
    a = a_ref[0]
    bits = lax.bitcast_convert_type(a, I32)
    capf = jnp.float32(cap)

    def count(mask):
        return jnp.sum(jnp.where(mask, 1.0, 0.0), keepdims=True)

    thr = jnp.zeros((1, 1), I32)
    for bit in range(30, -1, -1):
        cand = thr | (1 << bit)
        thr = jnp.where(count(bits >= cand) >= capf, cand, thr)

    upper = _bf01(_iota((LANES, LANES), 0) <= _iota((LANES, LANES), 1))
    ones_ll = _ones_bf16((LANES, LANES))
    strict_lower = _bf01(_iota((nrows, nrows), 1) < _iota((nrows, nrows), 0))

    def prefix(m01):
        within = _dot(m01, upper)
        tot = _dot(m01, ones_ll)
        before = _dot(strict_lower, tot.astype(BF16))
        return within + before

    gt = bits > thr
    eq = bits == thr
    need = capf - count(gt)
    eq_rank = prefix(_bf01(eq))
    mask = gt | (eq & (eq_rank <= need))
    m01 = _bf01(mask)
    incl = prefix(m01)
    pos_ref[0] = jnp.where(mask, incl - 1.0, -1.0).astype(I32)

    ones_8l = _ones_bf16((8, LANES))
    tot_row = _dot_nt(ones_8l, m01)
    strict_upper = _bf01(_iota((nrows, nrows), 0) < _iota((nrows, nrows), 1))
    start_row = _dot(tot_row.astype(BF16), strict_upper)
    end_row = start_row + tot_row
    rs_ref[0] = start_row[0:1, :].astype(I32)

    s_col = _iota((cap, nrows), 0).astype(F32)
    g01 = _bf01((start_row[0:1, :] <= s_col) & (s_col < end_row[0:1, :]))
    hi = jnp.floor(incl * (1.0 / 256.0))
    lo = incl - 256.0 * hi
    rank_at = 256.0 * _dot(g01, hi.astype(BF16)) + _dot(g01, lo.astype(BF16))
    s_lane = _iota((cap, LANES), 0).astype(F32)
    ind = _bf01(rank_at <= s_lane)
    local_row = _dot_nt(ones_8l, ind)
    rvals = _iota((8, nrows), 1).astype(F32).astype(BF16)
    row_of = _dot_nt(rvals, g01)
    idx_ref[0] = (row_of[0:1, :] * float(LANES) + local_row[0:1, :]).astype(I32)


def _route(aff_t3, cap):
    ne, nrows, _ = aff_t3.shape
    assert nrows <= 256 and cap % LANES == 0
    kern = functools.partial(_route_kernel, cap=cap, nrows=nrows)
    return pl.pallas_call(
        kern,
        grid=(ne,),
        in_specs=[pl.BlockSpec((1, nrows, LANES), lambda e: (e, 0, 0))],
        out_specs=[
            pl.BlockSpec((1, 1, cap), lambda e: (e, 0, 0)),
            pl.BlockSpec((1, nrows, LANES), lambda e: (e, 0, 0)),
            pl.BlockSpec((1, 1, nrows), lambda e: (e, 0, 0)),
        ],
        out_shape=[
            jax.ShapeDtypeStruct((ne, 1, cap), I32),
            jax.ShapeDtypeStruct((ne, nrows, LANES), I32),
            jax.ShapeDtypeStruct((ne, 1, nrows), I32),
        ],
        compiler_params=_cparams(("parallel",), 48),
        name="route",
    )(aff_t3)


def _expert_kernel(idx_ref, h2_hbm, wg_ref, wu_ref, wd_ref, ye_ref, buf, sem, *, ts, nreal, nfc):
    step = pl.program_id(0)
    slot = step % 2

    def row_copy(st, sl, r):
        t = idx_ref[st * ts + r]
        return pltpu.make_async_copy(h2_hbm.at[pl.ds(t, 1), :], buf.at[sl, pl.ds(r, 1), :], sem.at[sl])

    def wait_tile(sl):
        pltpu.make_async_copy(h2_hbm.at[pl.ds(0, ts), :], buf.at[sl], sem.at[sl]).wait()

    @pl.when(step == 0)
    def _():
        def body(r, carry):
            row_copy(0, 0, r).start()
            return carry
        lax.fori_loop(0, ts, body, 0)

    @pl.when(step < nreal)
    def _():
        wait_tile(slot)
        x = buf[slot].astype(BF16)
        nxt = jnp.minimum(step + 1, nreal - 1)
        rows_per = ts // nfc
        fc = wg_ref.shape[2] // nfc
        y = None
        for c in range(nfc):
            for r in range(c * rows_per, (c + 1) * rows_per):
                row_copy(nxt, 1 - slot, r).start()
            a = _dot(x, wg_ref[0, :, c * fc:(c + 1) * fc])
            b = _dot(x, wu_ref[0, :, c * fc:(c + 1) * fc])
            hm = (_silu(a) * b).astype(BF16)
            t = _dot(hm, wd_ref[0, c * fc:(c + 1) * fc, :])
            y = t if y is None else y + t
        ye_ref[...] = y.astype(BF16)

        @pl.when(step == nreal - 1)
        def _():
            wait_tile(1 - slot)

    @pl.when(step >= nreal)
    def _():
        ye_ref[...] = jnp.zeros_like(ye_ref)


def _experts(idx_flat, h2, w_gate, w_up, w_down, cap):
    n, d = h2.shape
    ne, _, f = w_gate.shape
    ts = min(256, cap)
    nj = cap // ts
    nreal = ne * nj
    nfc = 4
    assert ts % nfc == 0 and f % nfc == 0
    kern = functools.partial(_expert_kernel, ts=ts, nreal=nreal, nfc=nfc)
    w_idx = lambda s, idx: (jnp.minimum(s // nj, ne - 1), 0, 0)
    grid_spec = pltpu.PrefetchScalarGridSpec(
        num_scalar_prefetch=1,
        grid=(nreal + 1,),
        in_specs=[
            pl.BlockSpec(memory_space=pl.ANY),
            pl.BlockSpec((1, d, f), w_idx),
            pl.BlockSpec((1, d, f), w_idx),
            pl.BlockSpec((1, f, d), w_idx),
        ],
        out_specs=pl.BlockSpec((ts, d), lambda s, idx: (s, 0)),
        scratch_shapes=[pltpu.VMEM((2, ts, d), F32), pltpu.SemaphoreType.DMA((2,))],
    )
    return pl.pallas_call(
        kern,
        grid_spec=grid_spec,
        out_shape=jax.ShapeDtypeStruct((ne * cap + ts, d), BF16),
        compiler_params=_cparams(("arbitrary",), 56),
        name="experts",
    )(idx_flat, h2, w_gate, w_up, w_down)


def _combine_kernel(ts_ref, x1_ref, pos_ref, aff_ref, nf_ref, ye_hbm, y_ref, buf, sem,
                    *, tt, cap, ne, piece):
    i = pl.program_id(0)
    nsteps = pl.num_programs(0)
    slot = i % 2
    br = tt + BF16_ROWS
    pieces = [(0, piece), (piece, br - piece)]
    assert br > piece and piece % BF16_ROWS == 0

    def start_of(st, e):
        first = e * cap + ts_ref[st * ne + e]
        return pl.multiple_of(jnp.bitwise_and(first, -BF16_ROWS), BF16_ROWS)

    def rows_needed(st, e):
        first = e * cap + ts_ref[st * ne + e]
        return first - start_of(st, e) + ts_ref[(st + 1) * ne + e] - ts_ref[st * ne + e]

    def copy(st, sl, e, p):
        off, size = pieces[p]
        return pltpu.make_async_copy(ye_hbm.at[pl.ds(start_of(st, e) + off, size), :],
                                     buf.at[sl, pl.ds(e * br + off, size), :], sem.at[sl])

    def for_each_copy(st, sl, fn):
        for e in range(ne):
            fn(copy(st, sl, e, 0))
        for e in range(ne):
            for p in range(1, len(pieces)):
                @pl.when(rows_needed(st, e) > pieces[p][0])
                def _():
                    fn(copy(st, sl, e, p))

    @pl.when(i == 0)
    def _():
        buf[...] = jnp.zeros_like(buf)
        for_each_copy(0, 0, lambda c: c.start())

    @pl.when(i + 1 < nsteps)
    def _():
        for_each_copy(i + 1, 1 - slot, lambda c: c.start())

    for_each_copy(i, slot, lambda c: c.wait())

    pos = pos_ref[...]
    aff = aff_ref[...]
    lane_e = _iota((1, ne), 1)
    base = jnp.zeros((1, ne), I32)
    for e in range(ne):
        base = jnp.where(lane_e == e, start_of(i, e) - e * cap, base)
    rel = jnp.clip(pos - base, -1, br).astype(F32)
    kdim = ne * br
    c_idx = _iota((ne, kdim), 1)
    e_idx = _iota((ne, kdim), 0)
    expand = _bf01((c_idx >= e_idx * br) & (c_idx < (e_idx + 1) * br))
    e_of_c = jnp.sum(jnp.where(c_idx >= (e_idx + 1) * br, 1.0, 0.0), axis=0, keepdims=True)
    k_of_c = _iota((1, kdim), 1).astype(F32) - float(br) * e_of_c
    rel_x = _dot(rel.astype(BF16), expand)
    aff_x = _dot(aff.astype(BF16), expand)
    emat = jnp.where(rel_x == k_of_c, aff_x, 0.0).astype(BF16)
    out = x1_ref[...] + _dot(emat, buf[slot])
    ms = jnp.mean(out * out, axis=-1, keepdims=True)
    y_ref[...] = out * lax.rsqrt(ms + EPS) * nf_ref[...]


def _combine(tile_starts, x1, pos_tm, aff, norm_final, ye, cap):
    n, d = x1.shape
    ne = aff.shape[1]
    tt = LANES
    br = tt + BF16_ROWS
    assert cap % BF16_ROWS == 0 and ye.shape[0] >= ne * cap + br
    kern = functools.partial(_combine_kernel, tt=tt, cap=cap, ne=ne, piece=80)
    grid_spec = pltpu.PrefetchScalarGridSpec(
        num_scalar_prefetch=1,
        grid=(n // tt,),
        in_specs=[
            pl.BlockSpec((tt, d), lambda i, ts: (i, 0)),
            pl.BlockSpec((tt, ne), lambda i, ts: (i, 0)),
            pl.BlockSpec((tt, ne), lambda i, ts: (i, 0)),
            pl.BlockSpec((1, d), lambda i, ts: (0, 0)),
            pl.BlockSpec(memory_space=pl.ANY),
        ],
        out_specs=pl.BlockSpec((tt, d), lambda i, ts: (i, 0)),
        scratch_shapes=[pltpu.VMEM((2, ne * br, d), BF16), pltpu.SemaphoreType.DMA((2,))],
    )
    return pl.pallas_call(
        kern,
        grid_spec=grid_spec,
        out_shape=jax.ShapeDtypeStruct((n, d), F32),
        compiler_params=_cparams(("arbitrary",), 48),
        name="combine",
    )(tile_starts, x1, pos_tm, aff, norm_final, ye)


def _trunk(x, w, nh, width, wb_width):
    b, l, d = x.shape
    n = b * l
    ne = w["w_router"].shape[1]
    cap = max(1, CAPACITY_FACTOR * n // ne)

    p_big, ab = _inproj(x, w["norm_mix"], w["w_big"], w["w_small"], width)
    qkv = _prep(p_big, w["conv_a_w"], width)
    gcum, beta, gcum_t = _gates(ab, w["a_log"], w["dt_bias"], nh)
    o_f, o_b = _gdn2(qkv, gcum, beta, gcum_t, nh)
    merged = _mix(o_f, o_b, p_big, w["head_norm"], w["conv_b_w"], w["conv_b_b"], w["ln_b_g"], w["ln_b_b"],
                  w["w_proj_a"], w["w_proj_b"], width, wb_width, d)
    x1, h2, aff = _outproj(merged, x, w["w_out"], w["norm_ffn"], w["w_router"])

    aff = aff.reshape(n, ne)
    nrows = n // LANES
    idx, pos, row_start = _route(aff.T.reshape(ne, nrows, LANES), cap)
    idx_flat = idx.reshape(ne * cap)
    pos_tm = pos.reshape(ne, n).T
    tile_starts = jnp.concatenate(
        [row_start.reshape(ne, nrows).T, jnp.full((1, ne), cap, I32)], axis=0).reshape(-1)

    ye = _experts(idx_flat, h2.reshape(n, d), w["w_gate"], w["w_up"], w["w_down"], cap)
    y = _combine(tile_starts, x1.reshape(n, d), pos_tm, aff, w["norm_final"], ye, cap)
    return y.reshape(b, l, d)


def kernel(x_prompt, x_sample, norm_mix, w_in, conv_a_w, a_log, dt_bias, head_norm, w_proj_a,
           conv_b_w, conv_b_b, ln_b_g, ln_b_b, w_proj_b, w_out, norm_ffn, w_router, w_gate, w_up,
           w_down, norm_final):
    assert w_in.shape[0] == 1, "single layer"
    width = conv_a_w.shape[-1] // 3
    wb_width = conv_b_w.shape[-1]
    nh = width // HEAD_DIM
    off_small = 4 * width
    off_glu = off_small + 4 * nh
    w_in0 = w_in[0]
    row = lambda v: v.reshape(1, -1)
    w = {
        "norm_mix": row(norm_mix[0]),
        "w_big": jnp.concatenate([w_in0[:, :off_small], w_in0[:, off_glu:]], axis=1).astype(BF16),
        "w_small": w_in0[:, off_small:off_glu],
        "conv_a_w": conv_a_w[0],
        "a_log": a_log[0],
        "dt_bias": dt_bias[0],
        "head_norm": row(head_norm[0]),
        "w_proj_a": w_proj_a[0].astype(BF16),
        "conv_b_w": conv_b_w[0],
        "conv_b_b": row(conv_b_b[0]),
        "ln_b_g": row(ln_b_g[0]),
        "ln_b_b": row(ln_b_b[0]),
        "w_proj_b": w_proj_b[0].astype(BF16),
        "w_out": w_out[0].astype(BF16),
        "norm_ffn": row(norm_ffn[0]),
        "w_router": w_router[0],
        "w_gate": w_gate[0].astype(BF16),
        "w_up": w_up[0].astype(BF16),
        "w_down": w_down[0].astype(BF16),
        "norm_final": row(norm_final),
    }
    y_prompt = _trunk(x_prompt, w, nh, width, wb_width)
    y_sample = _trunk(x_sample, w, nh, width, wb_width)
    return (y_prompt, y_sample)
```

```python
import functools

import jax
import jax.numpy as jnp
from jax import lax
from jax.experimental import pallas as pl
from jax.experimental.pallas import tpu as pltpu

F32 = jnp.float32
BF16 = jnp.bfloat16
I32 = jnp.int32

HEAD_DIM = 128
CHUNK = 64
CHUNK_SHIFT = 6
CAPACITY_FACTOR = 2
EPS = 1e-6
LANES = 128
BF16_ROWS = 16
MIB = 1024 * 1024


def _cparams(sem, vmem_mib):
    return pltpu.CompilerParams(dimension_semantics=sem, vmem_limit_bytes=vmem_mib * MIB)


def _dot(a, b):
    return jnp.dot(a, b, preferred_element_type=F32)


def _dot_nt(a, b):
    return lax.dot_general(a, b, (((1,), (1,)), ((), ())), preferred_element_type=F32)


def _dot_tn(a, b):
    return lax.dot_general(a, b, (((0,), (0,)), ((), ())), preferred_element_type=F32)


def _split2(a):
    hi = a.astype(BF16)
    lo = (a - hi.astype(F32)).astype(BF16)
    return hi, lo


def _split3(a):
    hi = a.astype(BF16)
    r = a - hi.astype(F32)
    mid = r.astype(BF16)
    lo = (r - mid.astype(F32)).astype(BF16)
    return hi, mid, lo


def _dot_hi(a, b):
    ah, al = _split2(a)
    bh, bl = _split2(b)
    n = b.shape[1]
    both = _dot(ah, jnp.concatenate([bh, bl], axis=1))
    return both[:, :n] + both[:, n:] + _dot(al, bh)


def _sigmoid(x):
    return 0.5 * jnp.tanh(0.5 * x) + 0.5


def _silu(x):
    return x * _sigmoid(x)


def _softplus(x):
    return jnp.maximum(x, 0.0) + jnp.log(1.0 + jnp.exp(-jnp.abs(x)))


def _iota(shape, dim):
    return lax.broadcasted_iota(I32, shape, dim)


def _ones_bf16(shape):
    return jnp.ones(shape, BF16)


def _bf01(mask):
    return jnp.where(mask, 1.0, 0.0).astype(BF16)


def _inproj_kernel(x_ref, g_ref, wb_ref, ws_ref, p_ref, ab_ref, h_scr):
    @pl.when(pl.program_id(2) == 0)
    def _():
        rows = min(256, h_scr.shape[0])
        for r in range(h_scr.shape[0] // rows):
            x = x_ref[0, r * rows:(r + 1) * rows, :]
            ms = jnp.mean(x * x, axis=-1, keepdims=True)
            h = x * lax.rsqrt(ms + EPS) * g_ref[...]
            h_scr[r * rows:(r + 1) * rows, :] = h.astype(BF16)
            ab_ref[0, r * rows:(r + 1) * rows, :] = _dot_hi(h, ws_ref[...])

    p_ref[0] = _dot(h_scr[...], wb_ref[...]).astype(BF16)


def _inproj(x, g, w_big, w_small, tn):
    b, l, d = x.shape
    nbig = w_big.shape[1]
    nsmall = w_small.shape[1]
    tm = min(1024, l)
    return pl.pallas_call(
        _inproj_kernel,
        grid=(b, l // tm, nbig // tn),
        in_specs=[
            pl.BlockSpec((1, tm, d), lambda bi, i, j: (bi, i, 0)),
            pl.BlockSpec((1, d), lambda bi, i, j: (0, 0)),
            pl.BlockSpec((d, tn), lambda bi, i, j: (0, j)),
            pl.BlockSpec((d, nsmall), lambda bi, i, j: (0, 0)),
        ],
        out_specs=[
            pl.BlockSpec((1, tm, tn), lambda bi, i, j: (bi, i, j)),
            pl.BlockSpec((1, tm, nsmall), lambda bi, i, j: (bi, i, 0)),
        ],
        out_shape=[
            jax.ShapeDtypeStruct((b, l, nbig), BF16),
            jax.ShapeDtypeStruct((b, l, nsmall), F32),
        ],
        scratch_shapes=[pltpu.VMEM((tm, d), BF16)],
        compiler_params=_cparams(("parallel", "parallel", "arbitrary"), 48),
        name="inproj",
    )(x, g, w_big, w_small)


SUBLANES = 8


def _conv_shifts(ntaps):
    first = BF16_ROWS - ntaps // 2
    return sorted({(first + t) % SUBLANES for t in range(ntaps)})


def _conv_fill(scr, prev, main, nxt, i, last, tl, ntaps):
    hr = BF16_ROWS
    shifts = _conv_shifts(ntaps)
    rows = tl + 2 * hr
    base = shifts.index(0)
    scr[base, 0:hr, :] = jnp.where(i > 0, prev, 0.0)
    scr[base, hr:hr + tl, :] = main
    scr[base, hr + tl:rows, :] = jnp.where(i < last, nxt, 0.0)
    full = scr[base]
    for c, s in enumerate(shifts):
        if s:
            scr[c] = pltpu.roll(full, rows - s, axis=0)


def _conv_rows(scr, w, r0, rb, ntaps):
    shifts = _conv_shifts(ntaps)
    first = BF16_ROWS - ntaps // 2
    acc = None
    for t in range(ntaps):
        off = first + t
        s = off % SUBLANES
        win = scr[shifts.index(s), pl.ds(r0 + (off - s), rb), :]
        term = win * w[t:t + 1, :]
        acc = term if acc is None else acc + term
    return acc


def _prep_kernel(pm_ref, pp_ref, pn_ref, cw_ref, o_ref, scr, *, tl, rb, ntaps):
    i = pl.program_id(1)
    j = pl.program_id(2)
    last = pl.num_programs(1) - 1
    _conv_fill(scr, pp_ref[0].astype(F32), pm_ref[0].astype(F32), pn_ref[0].astype(F32),
               i, last, tl, ntaps)
    w = cw_ref[...]
    width = w.shape[1]
    nh = width // HEAD_DIM
    qscale = jnp.where(j == 0, HEAD_DIM ** -0.5, 1.0).astype(F32)
    is_qk = j < 2

    def body(r, carry):
        r0 = pl.multiple_of(r * rb, rb)
        y = _silu(_conv_rows(scr, w, r0, rb, ntaps))
        outs = []
        for h in range(nh):
            yh = y[:, h * HEAD_DIM:(h + 1) * HEAD_DIM]
            ss = jnp.sum(yh * yh, axis=-1, keepdims=True)
            fac = jnp.where(is_qk, lax.rsqrt(ss + EPS) * qscale, 1.0)
            outs.append(yh * fac)
        o_ref[0, 0, pl.ds(r0, rb), :] = jnp.concatenate(outs, axis=-1).astype(BF16)
        return carry

    lax.fori_loop(0, tl // rb, body, 0)


def _prep(p_big, conv_w, width):
    b, l, _ = p_big.shape
    tl = min(512, l)
    rb = 32
    hr = BF16_ROWS
    nhb = l // hr
    ntaps = conv_w.shape[0]
    assert ntaps // 2 < hr
    kern = functools.partial(_prep_kernel, tl=tl, rb=rb, ntaps=ntaps)
    return pl.pallas_call(
        kern,
        grid=(b, l // tl, 3),
        in_specs=[
            pl.BlockSpec((1, tl, width), lambda bi, i, j: (bi, i, j)),
            pl.BlockSpec((1, hr, width), lambda bi, i, j: (bi, jnp.maximum(i * (tl // hr) - 1, 0), j)),
            pl.BlockSpec((1, hr, width), lambda bi, i, j: (bi, jnp.minimum((i + 1) * (tl // hr), nhb - 1), j)),
            pl.BlockSpec((ntaps, width), lambda bi, i, j: (0, j)),
        ],
        out_specs=pl.BlockSpec((1, 1, tl, width), lambda bi, i, j: (j, bi, i, 0)),
        out_shape=jax.ShapeDtypeStruct((3, b, l, width), BF16),
        scratch_shapes=[pltpu.VMEM((len(_conv_shifts(ntaps)), tl + 2 * hr, width), F32)],
        compiler_params=_cparams(("parallel", "parallel", "arbitrary"), 32),
        name="prep",
    )(p_big, p_big, p_big, conv_w)


def _gates_kernel(ab_ref, at_ref, arow_ref, dtrow_ref, acol_ref, dtcol_ref,
                  gc_ref, bt_ref, gct_ref, *, tl, nh):
    ab = ab_ref[0]
    beta = _sigmoid(ab[:, :2 * nh])
    alpha = ab[:, 2 * nh:]
    g = -jnp.exp(arow_ref[...]) * _softplus(alpha + dtrow_ref[...])
    r = _iota((tl, tl), 0)
    c = _iota((tl, tl), 1)
    same = jnp.right_shift(r, CHUNK_SHIFT) == jnp.right_shift(c, CHUNK_SHIFT)
    m_f = _bf01(same & (c <= r))
    m_b = _bf01(same & (c >= r))
    g3 = _split3(g)
    gf = _dot(m_f, g3[0]) + _dot(m_f, g3[1]) + _dot(m_f, g3[2])
    gb = _dot(m_b, g3[0]) + _dot(m_b, g3[1]) + _dot(m_b, g3[2])
    gc_ref[0, 0] = gf[:, :nh]
    gc_ref[0, 1] = gb[:, nh:]
    bt_ref[0, 0] = beta[:, :nh]
    bt_ref[0, 1] = beta[:, nh:]
    nc = tl // CHUNK
    at = at_ref[0]
    gt = -jnp.exp(acol_ref[...])[None] * _softplus(at + dtcol_ref[...][None])
    gt2 = gt.reshape(nc * 2 * nh, CHUNK)
    rr = _iota((CHUNK, CHUNK), 0)
    cc = _iota((CHUNK, CHUNK), 1)
    u_f = _bf01(rr <= cc)
    u_b = _bf01(rr >= cc)
    t3 = _split3(gt2)
    cf = (_dot(t3[0], u_f) + _dot(t3[1], u_f) + _dot(t3[2], u_f)).reshape(nc, 2 * nh, CHUNK)
    cb = (_dot(t3[0], u_b) + _dot(t3[1], u_b) + _dot(t3[2], u_b)).reshape(nc, 2 * nh, CHUNK)
    gct_ref[0, 0] = cf[:, :nh, :]
    gct_ref[0, 1] = cb[:, nh:, :]


def _gates(ab, a_log, dt_bias, nh):
    b, l, _ = ab.shape
    tl = min(512, l)
    nc = tl // CHUNK
    at = ab[..., 2 * nh:].reshape(b, l // CHUNK, CHUNK, 2 * nh).swapaxes(-1, -2)
    arow = a_log.reshape(1, 2 * nh)
    dtrow = dt_bias.reshape(1, 2 * nh)
    acol = a_log.reshape(2 * nh, 1)
    dtcol = dt_bias.reshape(2 * nh, 1)
    kern = functools.partial(_gates_kernel, tl=tl, nh=nh)
    small = lambda shape: pl.BlockSpec(shape, lambda bi, i: (0, 0))
    return pl.pallas_call(
        kern,
        grid=(b, l // tl),
        in_specs=[
            pl.BlockSpec((1, tl, 4 * nh), lambda bi, i: (bi, i, 0)),
            pl.BlockSpec((1, nc, 2 * nh, CHUNK), lambda bi, i: (bi, i, 0, 0)),
            small((1, 2 * nh)), small((1, 2 * nh)), small((2 * nh, 1)), small((2 * nh, 1)),
        ],
        out_specs=[
            pl.BlockSpec((1, 2, tl, nh), lambda bi, i: (bi, 0, i, 0)),
            pl.BlockSpec((1, 2, tl, nh), lambda bi, i: (bi, 0, i, 0)),
            pl.BlockSpec((1, 2, nc, nh, CHUNK), lambda bi, i: (bi, 0, i, 0, 0)),
        ],
        out_shape=[
            jax.ShapeDtypeStruct((b, 2, l, nh), F32),
            jax.ShapeDtypeStruct((b, 2, l, nh), F32),
            jax.ShapeDtypeStruct((b, 2, l // CHUNK, nh, CHUNK), F32),
        ],
        compiler_params=_cparams(("parallel", "parallel"), 32),
        name="gates",
    )(ab, at, arow, dtrow, acol, dtcol)


def _gdn2_kernel(qf_ref, kf_ref, vf_ref, qb_ref, kb_ref, vb_ref,
                 gcf_ref, btf_ref, gctf_ref, gcb_ref, btb_ref, gctb_ref,
                 of_ref, ob_ref, s_scr, u_scr, wq_scr, ke_scr, qk_scr, *, tl, nh):
    i = pl.program_id(1)

    @pl.when(i == 0)
    def _():
        s_scr[...] = jnp.zeros_like(s_scr)

    nc = tl // CHUNK
    row = _iota((CHUNK, CHUNK), 0)
    col = _iota((CHUNK, CHUNK), 1)
    m_incl = [row >= col, row <= col]
    m_strict = [row > col, row < col]
    eye = jnp.where(row == col, 1.0, 0.0).astype(F32)
    refs = [dict(q=qf_ref, k=kf_ref, v=vf_ref, gc=gcf_ref, bt=btf_ref, gct=gctf_ref, o=of_ref),
            dict(q=qb_ref, k=kb_ref, v=vb_ref, gc=gcb_ref, bt=btb_ref, gct=gctb_ref, o=ob_ref)]
    streams = range(2 * nh)
    dr = [s // nh for s in streams]
    hd = [s % nh for s in streams]
    sl = [slice(hd[s] * HEAD_DIM, (hd[s] + 1) * HEAD_DIM) for s in streams]

    def chunk_rows(d, ci):
        c = ci if d == 0 else nc - 1 - ci
        return c, pl.multiple_of(c * CHUNK, CHUNK)

    def last_row(d, gcols):
        return gcols[CHUNK - 1:CHUNK, :] if d == 0 else gcols[0:1, :]

    def local_begin(ci):
        cr = [chunk_rows(d, ci) for d in range(2)]
        gcols = [refs[d]["gc"][0, 0, pl.ds(cr[d][1], CHUNK), :] for d in range(2)]
        bcols = [refs[d]["bt"][0, 0, pl.ds(cr[d][1], CHUNK), :] for d in range(2)]
        grows = [refs[d]["gct"][0, 0, cr[d][0]] for d in range(2)]
        glast = [last_row(d, gcols[d]) for d in range(2)]
        q = [refs[dr[s]]["q"][0, 0, pl.ds(cr[dr[s]][1], CHUNK), sl[s]] for s in streams]
        k16 = [refs[dr[s]]["k"][0, 0, pl.ds(cr[dr[s]][1], CHUNK), sl[s]] for s in streams]
        k = [k16[s].astype(F32) for s in streams]
        v = [refs[dr[s]]["v"][0, 0, pl.ds(cr[dr[s]][1], CHUNK), sl[s]].astype(F32) for s in streams]
        gcol = [gcols[dr[s]][:, hd[s]:hd[s] + 1] for s in streams]
        beta = [bcols[dr[s]][:, hd[s]:hd[s] + 1] for s in streams]
        grow = [grows[dr[s]][hd[s]:hd[s] + 1, :] for s in streams]
        gl = [glast[dr[s]][:, hd[s]:hd[s] + 1] for s in streams]
        kb = [k[s] * beta[s] for s in streams]
        kk = [_dot_nt(kb[s].astype(BF16), k16[s]) for s in streams]
        qk = [_dot_nt(q[s], k16[s]) for s in streams]
        return dict(grow=grow, q=q, k=k, v=v, gcol=gcol, beta=beta, gl=gl, kb=kb, kk=kk, qk=qk)

    def local_finish(st, slot):
        gcol, beta, kb = st["gcol"], st["beta"], st["kb"]
        dec = [jnp.where(m_incl[dr[s]],
                         jnp.exp(jnp.where(m_incl[dr[s]], gcol[s] - st["grow"][s], 0.0)), 0.0) for s in streams]
        pw = [jnp.where(m_strict[dr[s]], -(st["kk"][s] * dec[s]), 0.0) for s in streams]
        p16 = [pw[s].astype(BF16) for s in streams]
        tm = [eye + pw[s] for s in streams]
        for _ in range(5):
            pw = [_dot(p16[s], p16[s]) for s in streams]
            p16 = [pw[s].astype(BF16) for s in streams]
            tm = [tm[s] + _dot(tm[s].astype(BF16), p16[s]) for s in streams]
        eg = [jnp.exp(gcol[s]) for s in streams]
        rhs = [jnp.concatenate([st["v"][s] * beta[s], kb[s] * eg[s]], axis=-1).astype(BF16) for s in streams]
        sol = [_dot(tm[s].astype(BF16), rhs[s]) for s in streams]
        for s in streams:
            u_scr[slot, s] = sol[s][:, :HEAD_DIM]
            wq_scr[slot, s, 0:CHUNK, :] = sol[s][:, HEAD_DIM:].astype(BF16)
            wq_scr[slot, s, CHUNK:2 * CHUNK, :] = (st["q"][s].astype(F32) * eg[s]).astype(BF16)
            ke_scr[slot, s] = (st["k"][s] * jnp.exp(st["gl"][s] - gcol[s])).astype(BF16)
            qk_scr[slot, s] = (st["qk"][s] * dec[s]).astype(BF16)

    def scan_begin(slot):
        state = [s_scr[s] for s in streams]
        ws = [_dot(wq_scr[slot, s], state[s].astype(BF16)) for s in streams]
        return state, ws

    def scan_finish(ci, slot, state, ws):
        cr = [chunk_rows(d, ci) for d in range(2)]
        glast = [last_row(d, refs[d]["gc"][0, 0, pl.ds(cr[d][1], CHUNK), :]) for d in range(2)]
        vn16 = [(u_scr[slot, s] - ws[s][:CHUNK]).astype(BF16) for s in streams]
        s_add = [_dot_tn(ke_scr[slot, s], vn16[s]) for s in streams]
        o_in = [_dot(qk_scr[slot, s], vn16[s]) for s in streams]
        for s in streams:
            s_scr[s] = state[s] * jnp.exp(glast[dr[s]][:, hd[s]:hd[s] + 1]) + s_add[s]
            refs[dr[s]]["o"][0, pl.ds(cr[dr[s]][1], CHUNK), sl[s]] = (ws[s][CHUNK:] + o_in[s]).astype(BF16)

    def step(ci, slot):
        state, ws = scan_begin(slot)
        st = local_begin(jnp.minimum(ci + 1, nc - 1))
        scan_finish(ci, slot, state, ws)
        local_finish(st, 1 - slot)

    local_finish(local_begin(0), 0)

    def pair(j, carry):
        step(2 * j, 0)
        step(2 * j + 1, 1)
        return carry

    lax.fori_loop(0, nc // 2, pair, 0)


def _gdn2(qkv, gcum, beta, gcum_t, nh):
    _, b, l, width = qkv.shape
    tl = min(512, l)
    nt = l // tl
    nc = tl // CHUNK
    assert nc % 2 == 0
    kern = functools.partial(_gdn2_kernel, tl=tl, nh=nh)
    fw = lambda i: i
    bw = lambda i: nt - 1 - i
    qkv_spec = lambda which, t: pl.BlockSpec((1, 1, tl, width), lambda bi, i: (which, bi, t(i), 0))
    col_spec = lambda d, t: pl.BlockSpec((1, 1, tl, nh), lambda bi, i: (bi, d, t(i), 0))
    row_spec = lambda d, t: pl.BlockSpec((1, 1, nc, nh, CHUNK), lambda bi, i: (bi, d, t(i), 0, 0))
    ns = 2 * nh
    return pl.pallas_call(
        kern,
        grid=(b, nt),
        in_specs=[
            qkv_spec(0, fw), qkv_spec(1, fw), qkv_spec(2, fw),
            qkv_spec(0, bw), qkv_spec(1, bw), qkv_spec(2, bw),
            col_spec(0, fw), col_spec(0, fw), row_spec(0, fw),
            col_spec(1, bw), col_spec(1, bw), row_spec(1, bw),
        ],
        out_specs=[pl.BlockSpec((1, tl, width), lambda bi, i: (bi, i, 0)),
                   pl.BlockSpec((1, tl, width), lambda bi, i: (bi, nt - 1 - i, 0))],
        out_shape=[jax.ShapeDtypeStruct((b, l, width), BF16)] * 2,
        scratch_shapes=[
            pltpu.VMEM((ns, HEAD_DIM, HEAD_DIM), F32),
            pltpu.VMEM((2, ns, CHUNK, HEAD_DIM), F32),
            pltpu.VMEM((2, ns, 2 * CHUNK, HEAD_DIM), BF16),
            pltpu.VMEM((2, ns, CHUNK, HEAD_DIM), BF16),
            pltpu.VMEM((2, ns, CHUNK, CHUNK), BF16),
        ],
        compiler_params=_cparams(("parallel", "arbitrary"), 48),
        name="gdn",
    )(qkv, qkv, qkv, qkv, qkv, qkv, gcum, beta, gcum_t, gcum, beta, gcum_t)


def _mix_kernel(of_ref, ob_ref, z_ref, gm_ref, gp_ref, gn_ref, ga_ref, gb_ref,
                hn_ref, cw_ref, cb_ref, lg_ref, lb_ref, wa_ref, wb_ref,
                m_ref, scr, act, *, tl, rb, ntaps, wb_width):
    i = pl.program_id(1)
    last = pl.num_programs(1) - 1

    o = of_ref[0].astype(F32) + ob_ref[0].astype(F32)
    width = o.shape[1]
    nh = width // HEAD_DIM
    hn = hn_ref[...]
    parts = []
    for h in range(nh):
        oh = o[:, h * HEAD_DIM:(h + 1) * HEAD_DIM]
        ms = jnp.mean(oh * oh, axis=-1, keepdims=True)
        parts.append(oh * lax.rsqrt(ms + EPS) * hn)
    on = jnp.concatenate(parts, axis=-1)
    ya_in = (on * _silu(z_ref[0].astype(F32))).astype(BF16)
    y_a = _dot(ya_in, wa_ref[...])

    def glu(blk):
        blk = blk.astype(F32)
        return blk[:, :wb_width] * _sigmoid(blk[:, wb_width:])

    _conv_fill(scr, glu(gp_ref[0]), glu(gm_ref[0]), glu(gn_ref[0]), i, last, tl, ntaps)
    cw = cw_ref[...]
    cb = cb_ref[...]
    lg = lg_ref[...]
    lb = lb_ref[...]

    def body(r, carry):
        r0 = pl.multiple_of(r * rb, rb)
        acc = _conv_rows(scr, cw, r0, rb, ntaps) + cb
        mu = jnp.mean(acc, axis=-1, keepdims=True)
        xc = acc - mu
        var = jnp.mean(xc * xc, axis=-1, keepdims=True)
        y = xc * lax.rsqrt(var + EPS) * lg + lb
        act[pl.ds(r0, rb), :] = _silu(y).astype(BF16)
        return carry

    lax.fori_loop(0, tl // rb, body, 0)
    y_b = _dot(act[...], wb_ref[...])

    g_a = _sigmoid(ga_ref[0].astype(F32))
    g_b = _sigmoid(gb_ref[0].astype(F32))
    m_ref[0] = (g_a * y_a + g_b * y_b).astype(BF16)


def _mix(o_f, o_b, p_big, head_norm, conv_w, conv_b, ln_g, ln_b, w_a, w_b, width, wb_width, d):
    b, l, _ = o_f.shape
    tl = min(256, l)
    rb = 32
    hr = BF16_ROWS
    nhb = l // hr
    ntaps = conv_w.shape[0]
    z_blk = 3
    glu_blk = (4 * width) // (2 * wb_width)
    gate_blk = (4 * width + 2 * wb_width) // d
    assert (4 * width) % (2 * wb_width) == 0 and (4 * width + 2 * wb_width) % d == 0
    kern = functools.partial(_mix_kernel, tl=tl, rb=rb, ntaps=ntaps, wb_width=wb_width)
    const = lambda shape: pl.BlockSpec(shape, lambda bi, i: (0, 0))
    return pl.pallas_call(
        kern,
        grid=(b, l // tl),
        in_specs=[
            pl.BlockSpec((1, tl, width), lambda bi, i: (bi, i, 0)),
            pl.BlockSpec((1, tl, width), lambda bi, i: (bi, i, 0)),
            pl.BlockSpec((1, tl, width), lambda bi, i: (bi, i, z_blk)),
            pl.BlockSpec((1, tl, 2 * wb_width), lambda bi, i: (bi, i, glu_blk)),
            pl.BlockSpec((1, hr, 2 * wb_width),
                         lambda bi, i: (bi, jnp.maximum(i * (tl // hr) - 1, 0), glu_blk)),
            pl.BlockSpec((1, hr, 2 * wb_width),
                         lambda bi, i: (bi, jnp.minimum((i + 1) * (tl // hr), nhb - 1), glu_blk)),
            pl.BlockSpec((1, tl, d), lambda bi, i: (bi, i, gate_blk)),
            pl.BlockSpec((1, tl, d), lambda bi, i: (bi, i, gate_blk + 1)),
            const((1, HEAD_DIM)),
            const((ntaps, wb_width)), const((1, wb_width)), const((1, wb_width)), const((1, wb_width)),
            const((width, d)), const((wb_width, d)),
        ],
        out_specs=pl.BlockSpec((1, tl, d), lambda bi, i: (bi, i, 0)),
        out_shape=jax.ShapeDtypeStruct((b, l, d), BF16),
        scratch_shapes=[pltpu.VMEM((len(_conv_shifts(ntaps)), tl + 2 * hr, wb_width), F32),
                        pltpu.VMEM((tl, wb_width), BF16)],
        compiler_params=_cparams(("parallel", "parallel"), 48),
        name="mix",
    )(o_f, o_b, p_big, p_big, p_big, p_big, p_big, p_big,
      head_norm, conv_w, conv_b, ln_g, ln_b, w_a, w_b)


def _outproj_kernel(m_ref, x_ref, wo_ref, nf_ref, wr_ref, x1_ref, h2_ref, aff_ref):
    x1 = x_ref[0] + _dot(m_ref[0], wo_ref[...])
    x1_ref[0] = x1
    ms = jnp.mean(x1 * x1, axis=-1, keepdims=True)
    h2 = x1 * lax.rsqrt(ms + EPS) * nf_ref[...]
    h2_ref[0] = h2
    logits = _dot_hi(h2, wr_ref[...])
    mx = jnp.max(logits, axis=-1, keepdims=True)
    ex = jnp.exp(logits - mx)
    aff_ref[0] = ex / jnp.sum(ex, axis=-1, keepdims=True)


def _outproj(merged, x, w_out, norm_ffn, w_router):
    b, l, d = x.shape
    ne = w_router.shape[1]
    tl = min(256, l)
    const = lambda shape: pl.BlockSpec(shape, lambda bi, i: (0, 0))
    tok = lambda last: pl.BlockSpec((1, tl, last), lambda bi, i: (bi, i, 0))
    return pl.pallas_call(
        _outproj_kernel,
        grid=(b, l // tl),
        in_specs=[tok(d), tok(d), const((d, d)), const((1, d)), const((d, ne))],
        out_specs=[tok(d), tok(d), tok(ne)],
        out_shape=[
            jax.ShapeDtypeStruct((b, l, d), F32),
            jax.ShapeDtypeStruct((b, l, d), F32),
            jax.ShapeDtypeStruct((b, l, ne), F32),
        ],
        compiler_params=_cparams(("parallel", "parallel"), 48),
        name="outproj",
    )(merged, x, w_out, norm_ffn, w_router)


def _route_kernel(a_ref, idx_ref, pos_ref, rs_ref, *, cap, nrows):
    a = a_ref[0]
    bits = lax.bitcast_convert_type(a, I32)
    capf = jnp.float32(cap)

    def count(mask):
        return jnp.sum(jnp.where(mask, 1.0, 0.0), keepdims=True)

    thr = jnp.zeros((1, 1), I32)
    for bit in range(30, -1, -1):
        cand = thr | (1 << bit)
        thr = jnp.where(count(bits >= cand) >= capf, cand, thr)

    upper = _bf01(_iota((LANES, LANES), 0) <= _iota((LANES, LANES), 1))
    ones_ll = _ones_bf16((LANES, LANES))
    strict_lower = _bf01(_iota((nrows, nrows), 1) < _iota((nrows, nrows), 0))

    def prefix(m01):
        within = _dot(m01, upper)
        tot = _dot(m01, ones_ll)
        before = _dot(strict_lower, tot.astype(BF16))
        return within + before

    gt = bits > thr
    eq = bits == thr
    need = capf - count(gt)
    eq_rank = prefix(_bf01(eq))
    mask = gt | (eq & (eq_rank <= need))
    m01 = _bf01(mask)
    incl = prefix(m01)
    pos_ref[0] = jnp.where(mask, incl - 1.0, -1.0).astype(I32)

    ones_8l = _ones_bf16((8, LANES))
    tot_row = _dot_nt(ones_8l, m01)
    strict_upper = _bf01(_iota((nrows, nrows), 0) < _iota((nrows, nrows), 1))
    start_row = _dot(tot_row.astype(BF16), strict_upper)
    end_row = start_row + tot_row
    rs_ref[0] = start_row[0:1, :].astype(I32)

    s_col = _iota((cap, nrows), 0).astype(F32)
    g01 = _bf01((start_row[0:1, :] <= s_col) & (s_col < end_row[0:1, :]))
    hi = jnp.floor(incl * (1.0 / 256.0))
    lo = incl - 256.0 * hi
    rank_at = 256.0 * _dot(g01, hi.astype(BF16)) + _dot(g01, lo.astype(BF16))
    s_lane = _iota((cap, LANES), 0).astype(F32)
    ind = _bf01(rank_at <= s_lane)
    local_row = _dot_nt(ones_8l, ind)
    rvals = _iota((8, nrows), 1).astype(F32).astype(BF16)
    row_of = _dot_nt(rvals, g01)
    idx_ref[0] = (row_of[0:1, :] * float(LANES) + local_row[0:1, :]).astype(I32)


def _route(aff_t3, cap):
    ne, nrows, _ = aff_t3.shape
    assert nrows <= 256 and cap % LANES == 0
    kern = functools.partial(_route_kernel, cap=cap, nrows=nrows)
    return pl.pallas_call(
        kern,
        grid=(ne,),
        in_specs=[pl.BlockSpec((1, nrows, LANES), lambda e: (e, 0, 0))],
        out_specs=[
            pl.BlockSpec((1, 1, cap), lambda e: (e, 0, 0)),
            pl.BlockSpec((1, nrows, LANES), lambda e: (e, 0, 0)),
            pl.BlockSpec((1, 1, nrows), lambda e: (e, 0, 0)),
        ],
        out_shape=[
            jax.ShapeDtypeStruct((ne, 1, cap), I32),
            jax.ShapeDtypeStruct((ne, nrows, LANES), I32),
            jax.ShapeDtypeStruct((ne, 1, nrows), I32),
        ],
        compiler_params=_cparams(("parallel",), 48),
        name="route",
    )(aff_t3)


def _expert_kernel(idx_ref, h2_hbm, wg_ref, wu_ref, wd_ref, ye_ref, buf, sem, *, ts, nreal, nfc):
    step = pl.program_id(0)
    slot = step % 2

    def row_copy(st, sl, r):
        t = idx_ref[st * ts + r]
        return pltpu.make_async_copy(h2_hbm.at[pl.ds(t, 1), :], buf.at[sl, pl.ds(r, 1), :], sem.at[sl])

    def wait_tile(sl):
        pltpu.make_async_copy(h2_hbm.at[pl.ds(0, ts), :], buf.at[sl], sem.at[sl]).wait()

    @pl.when(step == 0)
    def _():
        def body(r, carry):
            row_copy(0, 0, r).start()
            return carry
        lax.fori_loop(0, ts, body, 0)

    @pl.when(step < nreal)
    def _():
        wait_tile(slot)
        x = buf[slot].astype(BF16)
        nxt = jnp.minimum(step + 1, nreal - 1)
        rows_per = ts // nfc
        fc = wg_ref.shape[2] // nfc
        y = None
        for c in range(nfc):
            for r in range(c * rows_per, (c + 1) * rows_per):
                row_copy(nxt, 1 - slot, r).start(priority=r % 2)
            a = _dot(x, wg_ref[0, :, c * fc:(c + 1) * fc])
            b = _dot(x, wu_ref[0, :, c * fc:(c + 1) * fc])
            hm = (_silu(a) * b).astype(BF16)
            t = _dot(hm, wd_ref[0, c * fc:(c + 1) * fc, :])
            y = t if y is None else y + t
        ye_ref[...] = y.astype(BF16)

        @pl.when(step == nreal - 1)
        def _():
            wait_tile(1 - slot)

    @pl.when(step >= nreal)
    def _():
        ye_ref[...] = jnp.zeros_like(ye_ref)


def _experts(idx_flat, h2, w_gate, w_up, w_down, cap):
    n, d = h2.shape
    ne, _, f = w_gate.shape
    ts = min(256, cap)
    nj = cap // ts
    nreal = ne * nj
    nfc = 4
    assert ts % nfc == 0 and f % nfc == 0
    kern = functools.partial(_expert_kernel, ts=ts, nreal=nreal, nfc=nfc)
    w_idx = lambda s, idx: (jnp.minimum(s // nj, ne - 1), 0, 0)
    grid_spec = pltpu.PrefetchScalarGridSpec(
        num_scalar_prefetch=1,
        grid=(nreal + 1,),
        in_specs=[
            pl.BlockSpec(memory_space=pl.ANY),
            pl.BlockSpec((1, d, f), w_idx),
            pl.BlockSpec((1, d, f), w_idx),
            pl.BlockSpec((1, f, d), w_idx),
        ],
        out_specs=pl.BlockSpec((ts, d), lambda s, idx: (s, 0)),
        scratch_shapes=[pltpu.VMEM((2, ts, d), F32), pltpu.SemaphoreType.DMA((2,))],
    )
    return pl.pallas_call(
        kern,
        grid_spec=grid_spec,
        out_shape=jax.ShapeDtypeStruct((ne * cap + ts, d), BF16),
        compiler_params=_cparams(("arbitrary",), 56),
        name="experts",
    )(idx_flat, h2, w_gate, w_up, w_down)


def _combine_kernel(ts_ref, x1_ref, pos_ref, aff_ref, nf_ref, ye_hbm, y_ref, buf, sem,
                    *, tt, cap, ne, piece):
    i = pl.program_id(0)
    nsteps = pl.num_programs(0)
    slot = i % 2
    br = tt + BF16_ROWS
    pieces = [(0, piece), (piece, piece), (2 * piece, br - 2 * piece)]
    assert br > 2 * piece and piece % BF16_ROWS == 0

    def start_of(st, e):
        first = e * cap + ts_ref[st * ne + e]
        return pl.multiple_of(jnp.bitwise_and(first, -BF16_ROWS), BF16_ROWS)

    def rows_needed(st, e):
        first = e * cap + ts_ref[st * ne + e]
        return first - start_of(st, e) + ts_ref[(st + 1) * ne + e] - ts_ref[st * ne + e]

    def copy(st, sl, e, p):
        off, size = pieces[p]
        return pltpu.make_async_copy(ye_hbm.at[pl.ds(start_of(st, e) + off, size), :],
                                     buf.at[sl, pl.ds(e * br + off, size), :], sem.at[sl])

    def for_each_copy(st, sl, fn):
        for e in range(ne):
            fn(copy(st, sl, e, 0))
        for e in range(ne):
            for p in range(1, len(pieces)):
                @pl.when(rows_needed(st, e) > pieces[p][0])
                def _():
                    fn(copy(st, sl, e, p))

    @pl.when(i == 0)
    def _():
        buf[...] = jnp.zeros_like(buf)
        for_each_copy(0, 0, lambda c: c.start())

    @pl.when(i + 1 < nsteps)
    def _():
        for_each_copy(i + 1, 1 - slot, lambda c: c.start())

    for_each_copy(i, slot, lambda c: c.wait())

    pos = pos_ref[...]
    aff = aff_ref[...]
    lane_e = _iota((1, ne), 1)
    base = jnp.zeros((1, ne), I32)
    for e in range(ne):
        base = jnp.where(lane_e == e, start_of(i, e) - e * cap, base)
    rel = jnp.clip(pos - base, -1, br).astype(F32)
    kdim = ne * br
    c_idx = _iota((ne, kdim), 1)
    e_idx = _iota((ne, kdim), 0)
    expand = _bf01((c_idx >= e_idx * br) & (c_idx < (e_idx + 1) * br))
    e_of_c = jnp.sum(jnp.where(c_idx >= (e_idx + 1) * br, 1.0, 0.0), axis=0, keepdims=True)
    k_of_c = _iota((1, kdim), 1).astype(F32) - float(br) * e_of_c
    rel_x = _dot(rel.astype(BF16), expand)
    aff_x = _dot(aff.astype(BF16), expand)
    emat = jnp.where(rel_x == k_of_c, aff_x, 0.0).astype(BF16)
    out = x1_ref[...] + _dot(emat, buf[slot])
    ms = jnp.mean(out * out, axis=-1, keepdims=True)
    y_ref[...] = out * lax.rsqrt(ms + EPS) * nf_ref[...]


def _combine(tile_starts, x1, pos_tm, aff, norm_final, ye, cap):
    n, d = x1.shape
    ne = aff.shape[1]
    tt = LANES
    br = tt + BF16_ROWS
    assert cap % BF16_ROWS == 0 and ye.shape[0] >= ne * cap + br
    kern = functools.partial(_combine_kernel, tt=tt, cap=cap, ne=ne, piece=64)
    grid_spec = pltpu.PrefetchScalarGridSpec(
        num_scalar_prefetch=1,
        grid=(n // tt,),
        in_specs=[
            pl.BlockSpec((tt, d), lambda i, ts: (i, 0)),
            pl.BlockSpec((tt, ne), lambda i, ts: (i, 0)),
            pl.BlockSpec((tt, ne), lambda i, ts: (i, 0)),
            pl.BlockSpec((1, d), lambda i, ts: (0, 0)),
            pl.BlockSpec(memory_space=pl.ANY),
        ],
        out_specs=pl.BlockSpec((tt, d), lambda i, ts: (i, 0)),
        scratch_shapes=[pltpu.VMEM((2, ne * br, d), BF16), pltpu.SemaphoreType.DMA((2,))],
    )
    return pl.pallas_call(
        kern,
        grid_spec=grid_spec,
        out_shape=jax.ShapeDtypeStruct((n, d), F32),
        compiler_params=_cparams(("arbitrary",), 48),
        name="combine",
    )(tile_starts, x1, pos_tm, aff, norm_final, ye)


def _trunk(x, w, nh, width, wb_width):
    b, l, d = x.shape
    n = b * l
    ne = w["w_router"].shape[1]
    cap = max(1, CAPACITY_FACTOR * n // ne)

    p_big, ab = _inproj(x, w["norm_mix"], w["w_big"], w["w_small"], width)
    qkv = _prep(p_big, w["conv_a_w"], width)
    gcum, beta, gcum_t = _gates(ab, w["a_log"], w["dt_bias"], nh)
    o_f, o_b = _gdn2(qkv, gcum, beta, gcum_t, nh)
    merged = _mix(o_f, o_b, p_big, w["head_norm"], w["conv_b_w"], w["conv_b_b"], w["ln_b_g"], w["ln_b_b"],
                  w["w_proj_a"], w["w_proj_b"], width, wb_width, d)
    x1, h2, aff = _outproj(merged, x, w["w_out"], w["norm_ffn"], w["w_router"])

    aff = aff.reshape(n, ne)
    nrows = n // LANES
    idx, pos, row_start = _route(aff.T.reshape(ne, nrows, LANES), cap)
    idx_flat = idx.reshape(ne * cap)
    pos_tm = pos.reshape(ne, n).T
    tile_starts = jnp.concatenate(
        [row_start.reshape(ne, nrows).T, jnp.full((1, ne), cap, I32)], axis=0).reshape(-1)

    ye = _experts(idx_flat, h2.reshape(n, d), w["w_gate"], w["w_up"], w["w_down"], cap)
    y = _combine(tile_starts, x1.reshape(n, d), pos_tm, aff, w["norm_final"], ye, cap)
    return y.reshape(b, l, d)


def kernel(x_prompt, x_sample, norm_mix, w_in, conv_a_w, a_log, dt_bias, head_norm, w_proj_a,
           conv_b_w, conv_b_b, ln_b_g, ln_b_b, w_proj_b, w_out, norm_ffn, w_router, w_gate, w_up,
           w_down, norm_final):
    assert w_in.shape[0] == 1, "single layer"
    width = conv_a_w.shape[-1] // 3
    wb_width = conv_b_w.shape[-1]
    nh = width // HEAD_DIM
    off_small = 4 * width
    off_glu = off_small + 4 * nh
    w_in0 = w_in[0]
    row = lambda v: v.reshape(1, -1)
    w = {
        "norm_mix": row(norm_mix[0]),
        "w_big": jnp.concatenate([w_in0[:, :off_small], w_in0[:, off_glu:]], axis=1).astype(BF16),
        "w_small": w_in0[:, off_small:off_glu],
        "conv_a_w": conv_a_w[0],
        "a_log": a_log[0],
        "dt_bias": dt_bias[0],
        "head_norm": row(head_norm[0]),
        "w_proj_a": w_proj_a[0].astype(BF16),
        "conv_b_w": conv_b_w[0],
        "conv_b_b": row(conv_b_b[0]),
        "ln_b_g": row(ln_b_g[0]),
        "ln_b_b": row(ln_b_b[0]),
        "w_proj_b": w_proj_b[0].astype(BF16),
        "w_out": w_out[0].astype(BF16),
        "norm_ffn": row(norm_ffn[0]),
        "w_router": w_router[0],
        "w_gate": w_gate[0].astype(BF16),
        "w_up": w_up[0].astype(BF16),
        "w_down": w_down[0].astype(BF16),
        "norm_final": row(norm_final),
    }
    y_prompt = _trunk(x_prompt, w, nh, width, wb_width)
    y_sample = _trunk(x_sample, w, nh, width, wb_width)
    return (y_prompt, y_sample)
```
